```python
import math
import jax, jax.numpy as jnp
from jax import lax
import numpy as np

D_MODEL = 1024
BATCH = 16
SEQ = 4096
DEPTH = 4
DEC_BATCH = 32
DEC_SEQ = 64
PAST_LEN = 4096

CHUNK = 64
QBLOCK = 128
HEAD_DIM = 64
N_EVEN = (DEPTH + 1) // 2
N_ODD = DEPTH // 2
N_HEADS_A = (D_MODEL // 2) // HEAD_DIM
N_HEADS_B = (D_MODEL // 2) // HEAD_DIM
N_HEADS_C = (D_MODEL // 2) // HEAD_DIM
N_HEADS_D = (D_MODEL // 2) // HEAD_DIM
W_A = N_HEADS_A * HEAD_DIM
W_B = N_HEADS_B * HEAD_DIM
W_C = N_HEADS_C * HEAD_DIM
W_D = N_HEADS_D * HEAD_DIM
IDX_HEADS = 8
IDX_DIM = 64
TOPK_MAX = 256
DECAY_LORA = 64
AAA_LORA = 64
GATE_LORA = 128
B_SIZES = (W_B, W_B, W_B, DECAY_LORA, AAA_LORA, GATE_LORA)
B_SHIFT_WIDTH = 3 * W_B + DECAY_LORA + AAA_LORA + GATE_LORA
P_EVEN = 4 * W_A + B_SHIFT_WIDTH
ODD_SIZES = (W_C, HEAD_DIM, HEAD_DIM, IDX_HEADS * IDX_DIM, IDX_DIM, IDX_HEADS, W_D, W_D, W_D)
P_ODD = W_C + 2 * HEAD_DIM + IDX_HEADS * IDX_DIM + IDX_DIM + IDX_HEADS + 3 * W_D
D_FF = -(-8 * D_MODEL // (3 * 256)) * 256
ALPHA = (2 * DEPTH) ** 0.25
BETA = (8 * DEPTH) ** -0.25
LN_EPS = 1e-5
GN_EPS = 1e-5
WKV_GN_EPS = 64e-5

kernel_name = "hybrid_streaming_retention_rwkv7_dsa_stickbreak"


def _split(p, sizes):
    outs, o = [], 0
    for s in sizes:
        outs.append(p[..., o:o + s])
        o += s
    return outs


def _heads(a, h):
    return a.reshape(*a.shape[:-1], h, a.shape[-1] // h)


def _layernorm(x, g, b):
    xf = x.astype(jnp.float32)
    mu = xf.mean(-1, keepdims=True)
    var = jnp.square(xf - mu).mean(-1, keepdims=True)
    return ((xf - mu) * lax.rsqrt(var + LN_EPS) * g + b).astype(x.dtype)


def _head_norm(y, g, eps):
    mu = y.mean(-1, keepdims=True)
    var = jnp.square(y - mu).mean(-1, keepdims=True)
    return (y - mu) * lax.rsqrt(var + eps) * g.reshape(y.shape[-2], y.shape[-1])


def _retention(q, k, v, s0):
    c = q.shape[2]
    log_g = jnp.log1p(-jnp.exp2(-5.0 - jnp.arange(N_HEADS_A, dtype=jnp.float32)))
    i = jnp.arange(c, dtype=jnp.float32)
    diff = i[:, None] - i[None, :]
    intra = jnp.where(diff >= 0, jnp.exp(log_g[:, None, None] * jnp.maximum(diff, 0.0)), 0.0)
    q_dec = jnp.exp(log_g[None, :] * (i[:, None] + 1.0))
    k_dec = jnp.exp(log_g[None, :] * (c - 1.0 - i[:, None]))
    chunk_dec = jnp.exp(log_g * c)

    def step(s, qkv):
        qc, kc, vc = qkv
        att = jnp.einsum('bihd,bjhd->bhij', qc, kc) * intra
        o = (jnp.einsum('bhij,bjhe->bihe', att, vc)
             + jnp.einsum('bihd,bhde->bihe', qc, s) * q_dec[None, :, :, None])
        s = s * chunk_dec[None, :, None, None] + jnp.einsum(
            'bjhd,bjhe->bhde', kc * k_dec[None, :, :, None], vc)
        return s, o

    xs = tuple(jnp.moveaxis(a, 1, 0) for a in (q, k, v))
    s, o = lax.scan(step, s0, xs)
    return jnp.moveaxis(o, 0, 1), s


def _wkv7(r, w, k, v, kk, a, s0):
    def step(s, inp):
        rt, wt, kt, vt, kkt, at = inp
        sa = jnp.einsum('bhvk,bhk->bhv', s, -kkt)
        s = (s * wt[:, :, None, :] + sa[..., None] * (kkt * at)[:, :, None, :]
             + vt[..., None] * kt[:, :, None, :])
        return s, jnp.einsum('bhvk,bhk->bhv', s, rt)

    xs = tuple(jnp.moveaxis(t, 1, 0) for t in (r, w, k, v, kk, a))
    s, y = lax.scan(step, s0, xs)
    return jnp.moveaxis(y, 0, 1), s


def _even_mixer(x, shift0, s_ret0, s_wkv0, n_chunks, w_in, w_out, gn_ret_g, mu, w0, w2, a0, a2,
                g2, k_k, k_a, r_k, gn_wkv_g):
    f32 = jnp.float32
    b, t, _ = x.shape
    p = x @ w_in
    pa, pb = p[..., :4 * W_A], p[..., 4 * W_A:]
    qa, ka, va, ga = _split(pa, (W_A, W_A, W_A, W_A))
    prev = jnp.concatenate([shift0.astype(pb.dtype), pb[:, :-1]], axis=1)
    rb, kb, vb, wl, al, gl = _split(pb + (prev - pb) * mu, B_SIZES)

    shp = (b, n_chunks, t // n_chunks, N_HEADS_A, HEAD_DIM)
    o_ret, s_ret = _retention(qa.astype(f32).reshape(shp),
                              ka.astype(f32).reshape(shp) * HEAD_DIM ** -0.5,
                              va.astype(f32).reshape(shp), s_ret0.astype(f32))
    o_ret = (_head_norm(o_ret.reshape(b, t, N_HEADS_A, HEAD_DIM), gn_ret_g, GN_EPS).reshape(b, t, W_A)
             * jax.nn.silu(ga.astype(f32)))

    w_log = -jax.nn.softplus(-(w0 + jnp.tanh(wl.astype(f32)) @ w2)) - 0.5
    decay = jnp.exp(-jnp.exp(w_log))
    a = jax.nn.sigmoid(a0 + al.astype(f32) @ a2)
    gate = jax.nn.sigmoid(gl.astype(f32)) @ g2
    kf = kb.astype(f32)
    kk = _heads(kf * k_k, N_HEADS_B)
    kk = kk * lax.rsqrt(jnp.maximum(jnp.sum(kk * kk, -1, keepdims=True), 1e-12))
    k4 = _heads(kf * (1.0 + (a - 1.0) * k_a), N_HEADS_B)
    r4 = _heads(rb.astype(f32), N_HEADS_B)
    v4 = _heads(vb.astype(f32), N_HEADS_B)
    y, s_wkv = _wkv7(r4, _heads(decay, N_HEADS_B), k4, v4, kk, _heads(a, N_HEADS_B),
                     s_wkv0.astype(f32))
    bonus = jnp.sum(r4 * k4 * r_k.reshape(N_HEADS_B, HEAD_DIM), -1, keepdims=True) * v4
    o_b = (_head_norm(y, gn_wkv_g, WKV_GN_EPS) + bonus).reshape(b, t, W_B) * gate

    out = jnp.concatenate([o_ret, o_b], axis=-1).astype(x.dtype) @ w_out
    return out, pb[:, -1:], s_ret, s_wkv


def _dsa_block(q, qi, wi, qpos, k, v, ki, kpos, topk, slopes):
    f32 = jnp.float32
    adm = (kpos[None, :] // CHUNK) <= (qpos[:, None] // CHUNK)
    rel = jax.nn.relu(jnp.einsum('bthd,bsd->bths', qi.astype(f32), ki.astype(f32)) * IDX_DIM ** -0.5)
    score = jnp.einsum('bths,bth->bts', rel, wi.astype(f32) * IDX_HEADS ** -0.5)
    score = jnp.where(adm[None], score, -jnp.inf)
    _, idx = lax.top_k(score, topk)
    gather = jax.vmap(lambda arr, ind: arr[ind])
    kg = gather(k, idx).astype(f32)
    vg = gather(v, idx).astype(f32)
    valid = (idx // CHUNK) <= (qpos[None, :, None] // CHUNK)
    dist = jnp.abs(qpos[None, :, None] - idx).astype(f32)
    logits = (jnp.einsum('bthd,btjd->bthj', q.astype(f32), kg) * HEAD_DIM ** -0.5
              - slopes[None, None, :, None] * dist[:, :, None, :])
    logits = jnp.where(valid[:, :, None, :], logits, -jnp.inf)
    p = jax.nn.softmax(logits, axis=-1)
    return jnp.einsum('bthj,btjd->bthd', p, vg)


def _stickbreak_block(q, qpos, k, v, kpos):
    f32 = jnp.float32
    z = jnp.einsum('bthd,bshd->bhts', q.astype(f32), k.astype(f32)) * HEAD_DIM ** -0.5
    causal = (kpos[None, :] < qpos[:, None])[None, None]
    log_rem = jnp.where(causal, jax.nn.log_sigmoid(-z), 0.0)
    after = lax.cumsum(log_rem, axis=3, reverse=True) - log_rem
    att = jnp.where(causal, jnp.exp(jax.nn.log_sigmoid(z) + after), 0.0)
    return jnp.einsum('bhts,bshd->bthd', att, v.astype(f32))


def _query_blocks(fn, qs, qpos):
    t = qpos.shape[0]
    if t <= QBLOCK:
        return fn(*qs, qpos)
    nb = t // QBLOCK

    def blk(a):
        return jnp.moveaxis(a.reshape(a.shape[0], nb, QBLOCK, *a.shape[2:]), 1, 0)

    outs = lax.map(lambda xs: fn(*xs), tuple(blk(a) for a in qs) + (qpos.reshape(nb, QBLOCK),))
    outs = jnp.moveaxis(outs, 0, 1)
    return outs.reshape(outs.shape[0], t, *outs.shape[3:])


def _odd_mixer(x, past_kc, past_vc, past_ki, past_kd, past_vd, w_in, w_out):
    b, t, _ = x.shape
    qc, kc, vc, qi, ki, wi, qd, kd, vd = _split(x @ w_in, ODD_SIZES)
    kd4 = _heads(kd, N_HEADS_D)
    vd4 = _heads(vd, N_HEADS_D)

    def cat(past, new):
        return jnp.concatenate([past.astype(new.dtype), new], axis=1)

    kc_all, vc_all, ki_all = cat(past_kc, kc), cat(past_vc, vc), cat(past_ki, ki)
    kd_all, vd_all = cat(past_kd, kd4), cat(past_vd, vd4)
    past = past_kc.shape[1]
    n_keys = past + t
    kpos = jnp.arange(n_keys)
    qpos = past + jnp.arange(t)
    topk = min(TOPK_MAX, n_keys // 4)
    slopes = jnp.exp2(-8.0 * (jnp.arange(N_HEADS_C, dtype=jnp.float32) + 1.0) / N_HEADS_C)

    def block(qc_b, qi_b, wi_b, qd_b, qpos_b):
        tq = qpos_b.shape[0]
        oc = _dsa_block(qc_b, qi_b, wi_b, qpos_b, kc_all, vc_all, ki_all, kpos, topk, slopes)
        od = _stickbreak_block(qd_b, qpos_b, kd_all, vd_all, kpos)
        return jnp.concatenate([oc.reshape(b, tq, W_C), od.reshape(b, tq, W_D)], axis=-1)

    o = _query_blocks(block, (_heads(qc, N_HEADS_C), qi.reshape(b, t, IDX_HEADS, IDX_DIM), wi,
                              _heads(qd, N_HEADS_D)), qpos)
    return o.astype(x.dtype) @ w_out, kc, vc, ki, kd4, vd4


def _swiglu(x, wg, wu, wd):
    return (jax.nn.silu(x @ wg) * (x @ wu)) @ wd


def setup_inputs(seed: int = 0) -> dict:
    key = jax.random.key(seed)
    ks = iter(jax.random.split(key, 64))
    f32 = jnp.float32

    def nrm(shape, scale):
        return jax.random.normal(next(ks), shape, f32) * scale

    def unif(shape, lo, hi):
        return jax.random.uniform(next(ks), shape, f32, lo, hi)

    return {
        "x_prompt": nrm((BATCH, SEQ, D_MODEL), 1.0),
        "x_sample": nrm((DEC_BATCH, DEC_SEQ, D_MODEL), 1.0),
        "state_shift": nrm((N_EVEN, DEC_BATCH, 1, B_SHIFT_WIDTH), 1.0),
        "state_ret": nrm((N_EVEN, DEC_BATCH, N_HEADS_A, HEAD_DIM, HEAD_DIM), 0.3),
        "state_wkv": nrm((N_EVEN, DEC_BATCH, N_HEADS_B, HEAD_DIM, HEAD_DIM), 0.3),
        "cache_dsa_k": nrm((N_ODD, DEC_BATCH, PAST_LEN, HEAD_DIM), 1.0),
        "cache_dsa_v": nrm((N_ODD, DEC_BATCH, PAST_LEN, HEAD_DIM), 1.0),
        "cache_idx_k": nrm((N_ODD, DEC_BATCH, PAST_LEN, IDX_DIM), 1.0),
        "cache_sb_k": nrm((N_ODD, DEC_BATCH, PAST_LEN, N_HEADS_D, HEAD_DIM), 1.0),
        "cache_sb_v": nrm((N_ODD, DEC_BATCH, PAST_LEN, N_HEADS_D, HEAD_DIM), 1.0),
        "w_in_even": nrm((N_EVEN, D_MODEL, P_EVEN), D_MODEL ** -0.5),
        "w_out_even": nrm((N_EVEN, W_A + W_B, D_MODEL), (W_A + W_B) ** -0.5 * BETA),
        "gn_ret_g": 1.0 + nrm((N_EVEN, W_A), 0.02),
        "rwkv_mu": unif((N_EVEN, B_SHIFT_WIDTH), 0.0, 1.0),
        "rwkv_w0": unif((N_EVEN, W_B), -6.0, -1.0),
        "rwkv_w2": nrm((N_EVEN, DECAY_LORA, W_B), 0.5 * DECAY_LORA ** -0.5),
        "rwkv_a0": nrm((N_EVEN, W_B), 0.1),
        "rwkv_a2": nrm((N_EVEN, AAA_LORA, W_B), 0.5 * AAA_LORA ** -0.5),
        "rwkv_g2": nrm((N_EVEN, GATE_LORA, W_B), GATE_LORA ** -0.5),
        "rwkv_kk": 0.85 + nrm((N_EVEN, W_B), 0.02),
        "rwkv_ka": 1.0 + nrm((N_EVEN, W_B), 0.02),
        "rwkv_rk": nrm((N_EVEN, W_B), 0.1),
        "gn_wkv_g": 1.0 + nrm((N_EVEN, W_B), 0.02),
        "w_in_odd": nrm((N_ODD, D_MODEL, P_ODD), D_MODEL ** -0.5),
        "w_out_odd": nrm((N_ODD, W_C + W_D, D_MODEL), (W_C + W_D) ** -0.5 * BETA),
        "ln1_g": 1.0 + nrm((DEPTH, D_MODEL), 0.02),
        "ln1_b": nrm((DEPTH, D_MODEL), 0.02),
        "ln2_g": 1.0 + nrm((DEPTH, D_MODEL), 0.02),
        "ln2_b": nrm((DEPTH, D_MODEL), 0.02),
        "ffn_wg": nrm((DEPTH, D_MODEL, D_FF), D_MODEL ** -0.5),
        "ffn_wu": nrm((DEPTH, D_MODEL, D_FF), D_MODEL ** -0.5),
        "ffn_wd": nrm((DEPTH, D_FF, D_MODEL), D_FF ** -0.5 * BETA),
    }


def reference(x_prompt, x_sample, state_shift, state_ret, state_wkv, cache_dsa_k, cache_dsa_v,
              cache_idx_k, cache_sb_k, cache_sb_v, w_in_even, w_out_even, gn_ret_g, rwkv_mu,
              rwkv_w0, rwkv_w2, rwkv_a0, rwkv_a2, rwkv_g2, rwkv_kk, rwkv_ka, rwkv_rk, gn_wkv_g,
              w_in_odd, w_out_odd, ln1_g, ln1_b, ln2_g, ln2_b, ffn_wg, ffn_wu, ffn_wd):
    f32 = jnp.float32
    xp, xs = x_prompt, x_sample
    bp, tp = xp.shape[0], xp.shape[1]
    p_even = [[], [], []]
    s_even = [[], [], []]
    p_odd = [[], [], [], [], []]
    s_odd = [[], [], [], [], []]
    for l in range(DEPTH):
        if l % 2 == 0:
            e = l // 2
            ew = (w_in_even[e], w_out_even[e], gn_ret_g[e], rwkv_mu[e], rwkv_w0[e], rwkv_w2[e],
                  rwkv_a0[e], rwkv_a2[e], rwkv_g2[e], rwkv_kk[e], rwkv_ka[e], rwkv_rk[e], gn_wkv_g[e])
            yp, *st_p = _even_mixer(
                xp, jnp.zeros((bp, 1, B_SHIFT_WIDTH), xp.dtype),
                jnp.zeros((bp, N_HEADS_A, HEAD_DIM, HEAD_DIM), f32),
                jnp.zeros((bp, N_HEADS_B, HEAD_DIM, HEAD_DIM), f32), tp // CHUNK, *ew)
            ys, *st_s = _even_mixer(xs, state_shift[e], state_ret[e], state_wkv[e], 1, *ew)
            for i in range(3):
                p_even[i].append(st_p[i])
                s_even[i].append(st_s[i])
        else:
            o = l // 2
            yp, *rows_p = _odd_mixer(
                xp, jnp.zeros((bp, 0, HEAD_DIM), xp.dtype), jnp.zeros((bp, 0, HEAD_DIM), xp.dtype),
                jnp.zeros((bp, 0, IDX_DIM), xp.dtype),
                jnp.zeros((bp, 0, N_HEADS_D, HEAD_DIM), xp.dtype),
                jnp.zeros((bp, 0, N_HEADS_D, HEAD_DIM), xp.dtype), w_in_odd[o], w_out_odd[o])
            ys, *rows_s = _odd_mixer(xs, cache_dsa_k[o], cache_dsa_v[o], cache_idx_k[o],
                                     cache_sb_k[o], cache_sb_v[o], w_in_odd[o], w_out_odd[o])
            for i in range(5):
                p_odd[i].append(rows_p[i])
                s_odd[i].append(rows_s[i])
        xp = _layernorm(ALPHA * xp + yp, ln1_g[l], ln1_b[l])
        xs = _layernorm(ALPHA * xs + ys, ln1_g[l], ln1_b[l])
        xp = _layernorm(ALPHA * xp + _swiglu(xp, ffn_wg[l], ffn_wu[l], ffn_wd[l]), ln2_g[l], ln2_b[l])
        xs = _layernorm(ALPHA * xs + _swiglu(xs, ffn_wg[l], ffn_wu[l], ffn_wd[l]), ln2_g[l], ln2_b[l])

    p_shift, p_ret, p_wkv = [jnp.stack(a) for a in p_even]
    s_shift, s_ret, s_wkv = [jnp.stack(a) for a in s_even]
    p_dsa_k, p_dsa_v, p_idx_k, p_sb_k, p_sb_v = [jnp.stack(a) for a in p_odd]
    s_dsa_k, s_dsa_v, s_idx_k, s_sb_k, s_sb_v = [jnp.stack(a) for a in s_odd]
    return (xp, xs, p_shift, p_ret, p_wkv, p_dsa_k, p_dsa_v, p_idx_k, p_sb_k, p_sb_v,
            s_shift, s_ret, s_wkv, s_dsa_k, s_dsa_v, s_idx_k, s_sb_k, s_sb_v)
```

```python
import functools
import math

import jax
import jax.numpy as jnp
from jax import lax
from jax.experimental import pallas as pl
from jax.experimental.pallas import tpu as pltpu

F32 = jnp.float32
BF16 = jnp.bfloat16
I32 = jnp.int32

D_MODEL = 1024
HEAD_DIM = 64
CHUNK = 64
N_HEADS = 8
N_PAIRS = N_HEADS // 2
W_MIX = N_HEADS * HEAD_DIM
IDX_HEADS = 8
IDX_DIM = 64
TOPK_MAX = 256
DECAY_LORA = 64
AAA_LORA = 64
GATE_LORA = 128
B_SHIFT_WIDTH = 3 * W_MIX + DECAY_LORA + AAA_LORA + GATE_LORA
DEPTH = 4
D_FF = 2816
ALPHA = (2 * DEPTH) ** 0.25
LN_EPS = 1e-5
GN_EPS = 1e-5
WKV_GN_EPS = 64e-5
LANES = 128
NEG_BIG = -1e30
VMEM_LIMIT = 56 * 1024 * 1024
INT_MIN = -2 ** 31


def _cparams(*sem):
    return pltpu.CompilerParams(dimension_semantics=sem, vmem_limit_bytes=VMEM_LIMIT)


def _dot(a, b):
    return jnp.dot(a.astype(BF16), b.astype(BF16), preferred_element_type=F32)


def _dot_nt(a, b):
    return lax.dot_general(a.astype(BF16), b.astype(BF16), (((1,), (1,)), ((), ())),
                           preferred_element_type=F32)


def _split_dot(x, m):
    hi = x.astype(BF16)
    lo = (x - hi.astype(F32)).astype(BF16)
    return (jnp.dot(hi, m, preferred_element_type=F32) + jnp.dot(lo, m, preferred_element_type=F32))


def _split_dot_left(m, x):
    hi = x.astype(BF16)
    lo = (x - hi.astype(F32)).astype(BF16)
    return (jnp.dot(m, hi, preferred_element_type=F32) + jnp.dot(m, lo, preferred_element_type=F32))


def _softplus(x):
    return jnp.maximum(x, 0.0) + jnp.log1p(jnp.exp(-jnp.abs(x)))


def _layernorm_rows(h, g, b):
    mu = jnp.mean(h, axis=-1, keepdims=True)
    d = h - mu
    var = jnp.mean(d * d, axis=-1, keepdims=True)
    return d * lax.rsqrt(var + LN_EPS) * g + b


def _proj_kernel(x_ref, w_ref, *o_refs):
    acc = jnp.dot(x_ref[...].astype(BF16), w_ref[...], preferred_element_type=F32)
    for o_ref in o_refs:
        o_ref[...] = acc.astype(o_ref.dtype)


def _proj(x, w, out_dtypes, tm, tn):
    m, k = x.shape
    n = w.shape[1]
    assert m % tm == 0 and n % tn == 0
    return pl.pallas_call(
        _proj_kernel,
        grid=(m // tm, n // tn),
        in_specs=[pl.BlockSpec((tm, k), lambda i, j: (i, 0)),
                  pl.BlockSpec((k, tn), lambda i, j: (0, j))],
        out_specs=[pl.BlockSpec((tm, tn), lambda i, j: (i, j)) for _ in out_dtypes],
        out_shape=[jax.ShapeDtypeStruct((m, n), dt) for dt in out_dtypes],
        compiler_params=_cparams("parallel", "arbitrary"),
    )(x, w)


def _outproj_ln_kernel(o1_ref, o2_ref, w1_ref, w2_ref, x_ref, g_ref, b_ref, y_ref):
    acc = jnp.dot(o1_ref[...], w1_ref[...], preferred_element_type=F32)
    acc = acc + jnp.dot(o2_ref[...], w2_ref[...], preferred_element_type=F32)
    y_ref[...] = _layernorm_rows(ALPHA * x_ref[...] + acc, g_ref[...], b_ref[...])


def _outproj_ln(o1, o2, w, x, g, b, tm):
    m, d = x.shape
    k1, k2 = o1.shape[1], o2.shape[1]
    row = lambda i: (i, 0)
    fix = lambda i: (0, 0)
    return pl.pallas_call(
        _outproj_ln_kernel,
        grid=(m // tm,),
        in_specs=[pl.BlockSpec((tm, k1), row), pl.BlockSpec((tm, k2), row),
                  pl.BlockSpec((k1, d), fix), pl.BlockSpec((k2, d), fix),
                  pl.BlockSpec((tm, d), row), pl.BlockSpec((1, d), fix), pl.BlockSpec((1, d), fix)],
        out_specs=pl.BlockSpec((tm, d), row),
        out_shape=jax.ShapeDtypeStruct((m, d), F32),
        compiler_params=_cparams("parallel"),
    )(o1, o2, w[:k1], w[k1:], x, g.reshape(1, d), b.reshape(1, d))


def _ffn_kernel(x_ref, wg_ref, wu_ref, wd_ref, g_ref, b_ref, y_ref, xb_s, acc_s):
    j = pl.program_id(1)

    @pl.when(j == 0)
    def _():
        xb_s[...] = x_ref[...].astype(BF16)
        acc_s[...] = jnp.zeros_like(acc_s)

    xb = xb_s[...]
    hg = jnp.dot(xb, wg_ref[...], preferred_element_type=F32)
    hu = jnp.dot(xb, wu_ref[...], preferred_element_type=F32)
    h = (hg * jax.nn.sigmoid(hg) * hu).astype(BF16)
    acc_s[...] += jnp.dot(h, wd_ref[...], preferred_element_type=F32)

    @pl.when(j == pl.num_programs(1) - 1)
    def _():
        y_ref[...] = _layernorm_rows(ALPHA * x_ref[...] + acc_s[...], g_ref[...], b_ref[...])


def _ffn_ln(x, wg, wu, wd, g, b, tm, tf):
    m, d = x.shape
    f = wg.shape[1]
    assert m % tm == 0 and f % tf == 0
    return pl.pallas_call(
        _ffn_kernel,
        grid=(m // tm, f // tf),
        in_specs=[pl.BlockSpec((tm, d), lambda i, j: (i, 0)),
                  pl.BlockSpec((d, tf), lambda i, j: (0, j)),
                  pl.BlockSpec((d, tf), lambda i, j: (0, j)),
                  pl.BlockSpec((tf, d), lambda i, j: (j, 0)),
                  pl.BlockSpec((1, d), lambda i, j: (0, 0)),
                  pl.BlockSpec((1, d), lambda i, j: (0, 0))],
        out_specs=pl.BlockSpec((tm, d), lambda i, j: (i, 0)),
        out_shape=jax.ShapeDtypeStruct((m, d), F32),
        scratch_shapes=[pltpu.VMEM((tm, d), BF16), pltpu.VMEM((tm, d), F32)],
        compiler_params=_cparams("parallel", "arbitrary"),
    )(x, wg, wu, wd, g.reshape(1, d), b.reshape(1, d))


def _even_kernel(pa_ref, pb_ref, shift0_ref, sret0_ref, swkv0_ref,
                 mu_ref, vec_ref, w2p_ref, a2p_ref, g2_ref, hsum_ref, lincl_ref, usuf_ref,
                 dmask_ref, qdec_ref, kdec_ref, cdec_ref,
                 oret_ref, ob_ref, sret_ref, swkv_ref, shift_ref,
                 sret_s, swkv_s, prev_s,
                 at_s, bt_s, kt_s, rt_s, bg_s, kg_s, v_s, gc_s, q_s, k_s, vr_s,
                 yret_s, ywkv_s, bonus_s, gate_s):
    c = pl.program_id(1)
    tb = pb_ref.shape[1]
    n_chunks = tb // CHUNK

    @pl.when(c == 0)
    def _():
        sret_s[...] = sret0_ref[0]
        swkv_s[...] = swkv0_ref[0]
        prev_s[...] = shift0_ref[0]

    hsum = hsum_ref[...]

    def vec(i):
        return vec_ref[i:i + 1, :]

    w0, a0, k_k, k_a, r_k, g_ret, g_wkv = (vec(i) for i in range(7))

    pb = pb_ref[0]
    row = lax.broadcasted_iota(I32, (tb, 1), 0)
    prev = jnp.where(row == 0, prev_s[...], pltpu.roll(pb, 1, axis=0))
    prev_s[...] = pb[tb - 1:tb, :]
    mixed = pb + (prev - pb) * mu_ref[...]
    rb = mixed[:, 0:W_MIX]
    kb = mixed[:, W_MIX:2 * W_MIX]
    vb = mixed[:, 2 * W_MIX:3 * W_MIX]
    wa = mixed[:, 3 * W_MIX:3 * W_MIX + LANES]
    gl = mixed[:, 3 * W_MIX + LANES:3 * W_MIX + 2 * LANES]
    lane = lax.broadcasted_iota(I32, (1, LANES), 1)
    wa_t = jnp.where(lane < DECAY_LORA, jnp.tanh(wa), wa).astype(BF16)
    w_log = -_softplus(-(w0 + jnp.dot(wa_t, w2p_ref[...], preferred_element_type=F32))) - 0.5
    logw = -jnp.exp(w_log)
    lr = jax.nn.sigmoid(a0 + jnp.dot(wa_t, a2p_ref[...], preferred_element_type=F32))
    gate_s[...] = jnp.dot(jax.nn.sigmoid(gl).astype(BF16), g2_ref[...], preferred_element_type=F32)
    kkv = kb * k_k
    kkn = kkv * lax.rsqrt(jnp.maximum(_split_dot(kkv * kkv, hsum), 1e-12))
    k4 = kb * (1.0 + (lr - 1.0) * k_a)
    bonus_s[...] = _split_dot(rb * k4 * r_k, hsum) * vb
    cum = _split_dot_left(lincl_ref[...], logw)
    suf = _split_dot_left(usuf_ref[...], logw)
    ginv = jnp.exp(-cum)
    gsuf = jnp.exp(suf)
    at_s[...] = (-kkn * jnp.exp(cum - logw)).astype(BF16)
    bt_s[...] = (kkn * lr * ginv).astype(BF16)
    kt_s[...] = (k4 * ginv).astype(BF16)
    rt_s[...] = (rb * jnp.exp(cum)).astype(BF16)
    bg_s[...] = (kkn * lr * gsuf).astype(BF16)
    kg_s[...] = (k4 * gsuf).astype(BF16)
    v_s[...] = vb.astype(BF16)
    gc_s[...] = jnp.exp(cum + suf)

    q_s[...] = pa_ref[0, :, 0:W_MIX].astype(BF16)
    k_s[...] = (pa_ref[0, :, W_MIX:2 * W_MIX] * HEAD_DIM ** -0.5).astype(BF16)
    vr_s[...] = pa_ref[0, :, 2 * W_MIX:3 * W_MIX].astype(BF16)

    head0 = lane < HEAD_DIM
    r2 = lax.broadcasted_iota(I32, (2 * CHUNK, 2 * CHUNK), 0)
    c2 = lax.broadcasted_iota(I32, (2 * CHUNK, 2 * CHUNK), 1)
    same = (r2 // CHUNK) == (c2 // CHUNK)
    strict = same & (r2 > c2)
    incl = same & (r2 >= c2)

    def chunk_body(ci, carry):
        r0 = pl.multiple_of(ci * CHUNK, CHUNK)
        rows = pl.ds(r0, CHUNK)
        for p in range(N_PAIRS):
            ls = slice(LANES * p, LANES * (p + 1))

            def two(ref):
                x = ref[rows, ls]
                z = jnp.zeros_like(x)
                return jnp.concatenate([jnp.where(head0, x, z), jnp.where(head0, z, x)], axis=0)

            a2, b2, k2, rr2, v2 = two(at_s), two(bt_s), two(kt_s), two(rt_s), two(v_s)
            bg2, kg2 = two(bg_s), two(kg_s)
            s = swkv_s[p]
            ar = jnp.concatenate([a2, rr2], axis=0)
            g = _dot_nt(ar, jnp.concatenate([b2, k2], axis=0))
            n_ab = jnp.where(strict, g[0:128, 0:128], 0.0)
            a_ak = jnp.where(strict, g[0:128, 128:256], 0.0)
            a_rb = jnp.where(incl, g[128:256, 0:128], 0.0)
            a_rk = jnp.where(incl, g[128:256, 128:256], 0.0)
            ps = _dot_nt(ar, s)
            u = ps[0:128] + _dot(a_ak, v2)
            npow = n_ab
            for lvl in range(6):
                u = u + _dot(npow, u)
                if lvl < 5:
                    npow = _dot(npow, npow)
            ub = u.astype(BF16)
            uv = jnp.concatenate([ub, v2], axis=0)
            y2 = ps[128:256] + _dot(jnp.concatenate([a_rb, a_rk], axis=1), uv)
            ywkv_s[rows, ls] = y2[0:CHUNK] + y2[CHUNK:2 * CHUNK]
            s_add = _dot(uv.astype(F32).T, jnp.concatenate([bg2, kg2], axis=0))
            swkv_s[p] = s * gc_s[pl.ds(r0, 1), ls] + s_add

            q2, kr2, vr2 = two(q_s), two(k_s), two(vr_s)
            sr = sret_s[p]
            att = _dot_nt(q2, kr2) * dmask_ref[p]
            o2 = _dot(att, vr2) + _dot(q2, sr) * qdec_ref[p]
            yret_s[rows, ls] = o2[0:CHUNK] + o2[CHUNK:2 * CHUNK]
            kd = (kr2.astype(F32) * kdec_ref[p])
            sret_s[p] = sr * cdec_ref[p] + _dot(kd.T, vr2)
        return carry

    lax.fori_loop(0, n_chunks, chunk_body, 0)

    inv = 1.0 / HEAD_DIM
    yr = yret_s[...]
    d = yr - _split_dot(yr, hsum) * inv
    var = _split_dot(d * d, hsum) * inv
    ga = pa_ref[0, :, 3 * W_MIX:4 * W_MIX]
    oret_ref[0] = (d * lax.rsqrt(var + GN_EPS) * g_ret * (ga * jax.nn.sigmoid(ga))).astype(oret_ref.dtype)
    yw = ywkv_s[...]
    d = yw - _split_dot(yw, hsum) * inv
    var = _split_dot(d * d, hsum) * inv
    ob_ref[0] = ((d * lax.rsqrt(var + WKV_GN_EPS) * g_wkv + bonus_s[...]) * gate_s[...]).astype(ob_ref.dtype)

    @pl.when(c == pl.num_programs(1) - 1)
    def _():
        sret_ref[0] = sret_s[...]
        swkv_ref[0] = swkv_s[...]
        shift_ref[0] = prev_s[...]


def _pair_blockdiag(s):
    b = s.shape[0]
    s = s.reshape(b, N_PAIRS, 2, HEAD_DIM, HEAD_DIM)
    z = jnp.zeros_like(s[:, :, 0])
    top = jnp.concatenate([s[:, :, 0], z], axis=-1)
    bot = jnp.concatenate([z, s[:, :, 1]], axis=-1)
    return jnp.concatenate([top, bot], axis=-2)


def _pair_unblock(s):
    b = s.shape[0]
    h0 = s[:, :, 0:HEAD_DIM, 0:HEAD_DIM]
    h1 = s[:, :, HEAD_DIM:, HEAD_DIM:]
    return jnp.stack([h0, h1], axis=2).reshape(b, N_HEADS, HEAD_DIM, HEAD_DIM)


def _retention_constants():
    log_g = jnp.log1p(-jnp.exp2(-5.0 - jnp.arange(N_HEADS, dtype=F32)))
    lg = jnp.repeat(log_g.reshape(N_PAIRS, 2), CHUNK, axis=1)
    t = jnp.tile(jnp.arange(CHUNK, dtype=F32), 2)
    hd = jnp.repeat(jnp.arange(2), CHUNK)
    diff = t[:, None] - t[None, :]
    same = hd[:, None] == hd[None, :]
    dmask = jnp.where(same[None] & (diff[None] >= 0),
                      jnp.exp(lg[:, :, None] * jnp.maximum(diff, 0.0)[None]), 0.0)
    ones = jnp.ones((1, 1, LANES), F32)
    qdec = jnp.exp(lg * (t[None] + 1.0))[:, :, None] * ones
    kdec = jnp.exp(lg * (CHUNK - 1.0 - t[None]))[:, :, None] * ones
    cdec = jnp.exp(lg * CHUNK)[:, :, None] * ones
    return dmask, qdec, kdec, cdec


def _chunk_sum_matrices(tb):
    r = jnp.arange(tb)
    same = (r[:, None] // CHUNK) == (r[None, :] // CHUNK)
    lincl = (same & (r[None, :] <= r[:, None])).astype(BF16)
    usuf = (same & (r[None, :] > r[:, None])).astype(BF16)
    return lincl, usuf


def _even_mixer(pa, pb, shift0, sret0, swkv0, prm, tb):
    b, t, _ = pa.shape
    assert t % tb == 0 and tb % CHUNK == 0
    lincl, usuf = _chunk_sum_matrices(tb)
    dmask, qdec, kdec, cdec = _retention_constants()
    hd = jnp.arange(W_MIX) // HEAD_DIM
    hsum = (hd[:, None] == hd[None, :]).astype(BF16)

    def fixed(shape):
        zero = (0,) * len(shape)
        return pl.BlockSpec(shape, lambda i, c: zero)

    def per_batch(shape):
        zero = (0,) * len(shape)
        return pl.BlockSpec((1,) + shape, lambda i, c: (i,) + zero)

    def per_block(w):
        return pl.BlockSpec((1, tb, w), lambda i, c: (i, c, 0))

    pair_state = (N_PAIRS, LANES, LANES)
    wide = lambda dt: pltpu.VMEM((tb, W_MIX), dt)
    outs = pl.pallas_call(
        _even_kernel,
        grid=(b, t // tb),
        in_specs=[per_block(4 * W_MIX), per_block(B_SHIFT_WIDTH),
                  per_batch((1, B_SHIFT_WIDTH)), per_batch(pair_state), per_batch(pair_state),
                  fixed((1, B_SHIFT_WIDTH)), fixed((8, W_MIX)),
                  fixed((LANES, W_MIX)), fixed((LANES, W_MIX)), fixed((GATE_LORA, W_MIX)),
                  fixed((W_MIX, W_MIX)), fixed((tb, tb)), fixed((tb, tb)),
                  fixed(pair_state), fixed(pair_state), fixed(pair_state), fixed(pair_state)],
        out_specs=[per_block(W_MIX), per_block(W_MIX),
                   per_batch(pair_state), per_batch(pair_state), per_batch((1, B_SHIFT_WIDTH))],
        out_shape=[jax.ShapeDtypeStruct((b, t, W_MIX), BF16), jax.ShapeDtypeStruct((b, t, W_MIX), BF16),
                   jax.ShapeDtypeStruct((b,) + pair_state, F32), jax.ShapeDtypeStruct((b,) + pair_state, F32),
                   jax.ShapeDtypeStruct((b, 1, B_SHIFT_WIDTH), F32)],
        scratch_shapes=[pltpu.VMEM(pair_state, F32), pltpu.VMEM(pair_state, F32),
                        pltpu.VMEM((1, B_SHIFT_WIDTH), F32)]
                       + [wide(BF16)] * 7 + [wide(F32)] + [wide(BF16)] * 3 + [wide(F32)] * 4,
        compiler_params=_cparams("parallel", "arbitrary"),
    )(pa, pb, shift0, _pair_blockdiag(sret0), _pair_blockdiag(swkv0),
      prm["mu"], prm["vec"], prm["w2p"], prm["a2p"], prm["g2"], hsum, lincl, usuf,
      dmask, qdec, kdec, cdec)
    o_ret, o_b, sret, swkv, shift = outs
    return o_ret, o_b, shift, _pair_unblock(sret), _pair_unblock(swkv)


def _dsa_kernel(qc_ref, qi_ref, wi_ref, kc_ref, vc_ref, ki_ref, slope_ref, o_ref,
                key_s, m_s, l_s, acc_s, *, tq, tk, past, topk):
    i = pl.program_id(1)
    q0 = past + i * tq
    n_kb = (q0 + tq + tk - 1) // tk

    qpos = q0 + lax.broadcasted_iota(I32, (tq, 1), 0)
    qchunk = qpos // CHUNK
    lane_k = lax.broadcasted_iota(I32, (1, tk), 1)

    wi = wi_ref[0] * (IDX_DIM ** -0.5 * IDX_HEADS ** -0.5)
    qi = qi_ref[0]
    qi_h = [qi[:, IDX_DIM * h:IDX_DIM * (h + 1)] for h in range(IDX_HEADS)]
    wi_h = [wi[:, h:h + 1] for h in range(IDX_HEADS)]

    def score_body(kb, carry):
        k0 = pl.multiple_of(kb * tk, tk)
        ki = ki_ref[0, pl.ds(k0, tk), :]
        s = jnp.zeros((tq, tk), F32)
        for h in range(IDX_HEADS):
            s = s + jnp.maximum(_dot_nt(qi_h[h], ki), 0.0) * wi_h[h]
        s = s + 0.0
        adm = ((k0 + lane_k) // CHUNK) <= qchunk
        s = jnp.where(adm, s, -jnp.inf)
        bits = pltpu.bitcast(s, I32)
        key_s[:, pl.ds(k0, tk)] = bits ^ ((bits >> 31) & 0x7FFFFFFF)
        return carry

    lax.fori_loop(0, n_kb, score_body, 0)

    def count(pred):
        def body(kb, acc):
            k0 = pl.multiple_of(kb * tk, tk)
            hit = jnp.where(pred(key_s[:, pl.ds(k0, tk)], k0 + lane_k), 1.0, 0.0)
            for g in range(tk // LANES):
                acc = acc + hit[:, LANES * g:LANES * (g + 1)]
            return acc
        acc = lax.fori_loop(0, n_kb, body, jnp.zeros((tq, LANES), F32))
        return jnp.sum(acc, axis=1, keepdims=True)

    base = jnp.where(count(lambda key, kp: key >= 0) >= topk, 0, INT_MIN).astype(I32)

    def bit_body(b, base):
        cand = base | (jnp.int32(1) << (30 - b))
        return jnp.where(count(lambda key, kp: key >= cand) >= topk, cand, base)

    thr = lax.fori_loop(0, 31, bit_body, base)
    need = topk - count(lambda key, kp: key > thr)
    n_eq = count(lambda key, kp: key == thr)

    def tie_cut(_):
        def tbit(b, m):
            cand = m | (jnp.int32(1) << (14 - b))
            return jnp.where(count(lambda key, kp: (key == thr) & (kp < cand)) < need, cand, m)
        return lax.fori_loop(0, 15, tbit, jnp.zeros((tq, 1), I32))

    last_eq = lax.cond(jnp.max(n_eq - need) > 0.0, tie_cut,
                       lambda _: jnp.full((tq, 1), 2 ** 30, I32), 0)

    qc = qc_ref[0]
    qc_h = [qc[:, HEAD_DIM * h:HEAD_DIM * (h + 1)] for h in range(N_HEADS)]
    m_s[...] = jnp.full(m_s.shape, NEG_BIG, F32)
    l_s[...] = jnp.zeros(l_s.shape, F32)
    acc_s[...] = jnp.zeros(acc_s.shape, F32)

    def att_body(kb, carry):
        k0 = pl.multiple_of(kb * tk, tk)
        kpos = k0 + lane_k
        key = key_s[:, pl.ds(k0, tk)]
        sel = ((kpos // CHUNK) <= qchunk) & ((key > thr) | ((key == thr) & (kpos <= last_eq)))
        mbias = jnp.where(sel, 0.0, NEG_BIG)
        ndist = -jnp.abs(qpos - kpos).astype(F32)
        kc = kc_ref[0, pl.ds(k0, tk), :]
        vc = vc_ref[0, pl.ds(k0, tk), :]
        for h in range(N_HEADS):
            logit = _dot_nt(qc_h[h], kc) + (slope_ref[h] * ndist + mbias)
            m_old = m_s[h]
            m_new = jnp.maximum(m_old, jnp.max(logit, axis=1, keepdims=True))
            p = jnp.where(sel, jnp.exp(logit - m_new), 0.0)
            alpha = jnp.exp(m_old - m_new)
            l_s[h] = alpha * l_s[h] + jnp.sum(p, axis=1, keepdims=True)
            acc_s[h] = alpha * acc_s[h] + _dot(p, vc)
            m_s[h] = m_new
        return carry

    lax.fori_loop(0, n_kb, att_body, 0)
    o_ref[0] = jnp.concatenate([acc_s[h] / l_s[h] for h in range(N_HEADS)], axis=1).astype(o_ref.dtype)


def _dsa(qc, qi, wi, kc, vc, ki, past, tq, tk):
    b, t, _ = qc.shape
    lp = kc.shape[1]
    n_keys = past + t
    topk = min(TOPK_MAX, n_keys // 4)
    assert t % tq == 0 and tq % CHUNK == 0 and lp % tk == 0 and lp >= n_keys
    slopes = jnp.exp2(-8.0 * (jnp.arange(N_HEADS, dtype=F32) + 1.0) / N_HEADS)
    qblk = lambda w: pl.BlockSpec((1, tq, w), lambda bi, i: (bi, i, 0))
    kblk = pl.BlockSpec((1, lp, HEAD_DIM), lambda bi, i: (bi, 0, 0))
    return pl.pallas_call(
        functools.partial(_dsa_kernel, tq=tq, tk=tk, past=past, topk=topk),
        grid=(b, t // tq),
        in_specs=[qblk(W_MIX), qblk(W_MIX), qblk(IDX_HEADS), kblk, kblk, kblk,
                  pl.BlockSpec(memory_space=pltpu.SMEM)],
        out_specs=qblk(W_MIX),
        out_shape=jax.ShapeDtypeStruct((b, t, W_MIX), BF16),
        scratch_shapes=[pltpu.VMEM((tq, lp), I32), pltpu.VMEM((N_HEADS, tq, 1), F32),
                        pltpu.VMEM((N_HEADS, tq, 1), F32), pltpu.VMEM((N_HEADS, tq, HEAD_DIM), F32)],
        compiler_params=_cparams("parallel", "arbitrary"),
    )(qc, qi, wi, kc, vc, ki, slopes)


def _sb_kernel(q_ref, k_ref, v_ref, o_ref, *, tq, tk, past):
    i = pl.program_id(2)
    q0 = past + i * tq
    qpos = q0 + lax.broadcasted_iota(I32, (tq, 1), 0)
    lane_k = lax.broadcasted_iota(I32, (1, tk), 1)
    lane = lax.broadcasted_iota(I32, (1, LANES), 1)
    head0 = lane < HEAD_DIM
    rk = lax.broadcasted_iota(I32, (tk, tk), 0)
    ck = lax.broadcasted_iota(I32, (tk, tk), 1)
    later = (rk > ck).astype(BF16)

    q = q_ref[0]
    zq = jnp.zeros_like(q)
    q_heads = (jnp.where(head0, q, zq), jnp.where(head0, zq, q))

    j_hi = (q0 + tq - 2) // tk
    n_full = q0 // tk

    def block(j, carry, masked):
        k0 = pl.multiple_of(j * tk, tk)
        kblk = k_ref[0, pl.ds(k0, tk), :]
        vblk = v_ref[0, pl.ds(k0, tk), :]
        new = []
        for h in range(2):
            run, acc = carry[h]
            z = _dot_nt(q_heads[h], kblk)
            sp = _softplus(z)
            if masked:
                causal = (k0 + lane_k) < qpos
                lrem = jnp.where(causal, -sp, 0.0)
            else:
                lrem = -sp
            after = _split_dot(lrem, later) + run
            att = jnp.exp(z - sp + after)
            if masked:
                att = jnp.where(causal, att, 0.0)
            acc = acc + _dot(att, vblk)
            run = run + jnp.sum(lrem, axis=1, keepdims=True)
            new.append((run, acc))
        return tuple(new)

    init = tuple((jnp.zeros((tq, 1), F32), jnp.zeros((tq, LANES), F32)) for _ in range(2))
    n_masked = j_hi - n_full + 1
    carry = lax.fori_loop(0, n_masked, lambda s, c: block(j_hi - s, c, True), init)
    carry = lax.fori_loop(0, n_full, lambda s, c: block(n_full - 1 - s, c, False), carry)
    o_ref[0] = jnp.where(head0, carry[0][1], carry[1][1]).astype(o_ref.dtype)


def _stickbreak(q, k, v, past, tq, tk):
    b, t, _ = q.shape
    lp = k.shape[1]
    assert t % tq == 0 and lp % tk == 0 and lp >= past + t
    kv = pl.BlockSpec((1, lp, LANES), lambda bi, p, i: (bi, 0, p))
    qo = pl.BlockSpec((1, tq, LANES), lambda bi, p, i: (bi, i, p))
    return pl.pallas_call(
        functools.partial(_sb_kernel, tq=tq, tk=tk, past=past),
        grid=(b, N_PAIRS, t // tq),
        in_specs=[qo, kv, kv],
        out_specs=qo,
        out_shape=jax.ShapeDtypeStruct((b, t, W_MIX), BF16),
        compiler_params=_cparams("parallel", "parallel", "arbitrary"),
    )(q, k, v)


def _row_tile(m, cap):
    t = cap
    while m % t:
        t //= 2
    return t


def _even_layer(x, shift0, sret0, swkv0, w_in, w_out, prm, tb):
    b, t, d = x.shape
    xf = x.reshape(b * t, d)
    tm = _row_tile(b * t, 512)
    (pa,) = _proj(xf, w_in[:, :4 * W_MIX], (F32,), tm, 512)
    (pb,) = _proj(xf, w_in[:, 4 * W_MIX:], (F32,), tm, 256)
    o_ret, o_b, shift, sret, swkv = _even_mixer(pa.reshape(b, t, -1), pb.reshape(b, t, -1),
                                                shift0, sret0, swkv0, prm, tb)
    return o_ret.reshape(b * t, -1), o_b.reshape(b * t, -1), w_out, (shift, sret, swkv)


def _pad_rows(a, lp):
    return jnp.pad(a, ((0, 0), (0, lp - a.shape[1]), (0, 0)))


def _odd_layer(x, caches, w_in_main, w_in_small, w_out, tq_dsa, tq_sb, tk_dsa, tk_sb):
    b, t, d = x.shape
    xf = x.reshape(b * t, d)
    tm = _row_tile(b * t, 512)
    pm32, pm16 = _proj(xf, w_in_main, (F32, BF16), tm, 512)
    (ps32,) = _proj(xf, w_in_small, (F32,), tm, 256)
    pm32 = pm32.reshape(b, t, -1)
    pm16 = pm16.reshape(b, t, -1)
    ps32 = ps32.reshape(b, t, -1)
    kc, vc, ki = (ps32[..., 64 * n:64 * (n + 1)] for n in range(3))
    wi = ps32[..., 192:192 + IDX_HEADS]
    kd = pm32[..., 3 * W_MIX:4 * W_MIX]
    vd = pm32[..., 4 * W_MIX:5 * W_MIX]
    qc = (pm32[..., 0:W_MIX] * HEAD_DIM ** -0.5).astype(BF16)
    qi = pm16[..., W_MIX:2 * W_MIX]
    qd = (pm32[..., 2 * W_MIX:3 * W_MIX] * HEAD_DIM ** -0.5).astype(BF16)
    new_rows = (kc, vc, ki, kd.reshape(b, t, N_HEADS, HEAD_DIM), vd.reshape(b, t, N_HEADS, HEAD_DIM))
    if caches is None:
        past = 0
        alls = [a.astype(BF16) for a in (kc, vc, ki, kd, vd)]
    else:
        past = caches[0].shape[1]
        olds = [c.reshape(b, past, -1).astype(BF16) for c in caches]
        alls = [jnp.concatenate([o, n.astype(BF16)], axis=1) for o, n in zip(olds, (kc, vc, ki, kd, vd))]
    n_keys = past + t
    lp_dsa = -(-n_keys // tk_dsa) * tk_dsa
    lp_sb = -(-n_keys // tk_sb) * tk_sb
    kc_a, vc_a, ki_a = (_pad_rows(a, lp_dsa) for a in alls[:3])
    kd_a, vd_a = (_pad_rows(a, lp_sb) for a in alls[3:])
    oc = _dsa(qc, qi, wi, kc_a, vc_a, ki_a, past, tq_dsa, tk_dsa)
    od = _stickbreak(qd, kd_a, vd_a, past, tq_sb, tk_sb)
    return oc.reshape(b * t, -1), od.reshape(b * t, -1), w_out, new_rows


def kernel(x_prompt, x_sample, state_shift, state_ret, state_wkv, cache_dsa_k, cache_dsa_v, cache_idx_k,
           cache_sb_k, cache_sb_v, w_in_even, w_out_even, gn_ret_g, rwkv_mu, rwkv_w0, rwkv_w2, rwkv_a0,
           rwkv_a2, rwkv_g2, rwkv_kk, rwkv_ka, rwkv_rk, gn_wkv_g, w_in_odd, w_out_odd, ln1_g, ln1_b,
           ln2_g, ln2_b, ffn_wg, ffn_wu, ffn_wd):
    xp, xs = x_prompt, x_sample
    bp, tp, d = xp.shape
    bs, ts, _ = xs.shape
    p_even, s_even = [[], [], []], [[], [], []]
    p_odd, s_odd = [[] for _ in range(5)], [[] for _ in range(5)]

    for l in range(DEPTH):
        if l % 2 == 0:
            e = l // 2
            zpad = jnp.zeros((DECAY_LORA, W_MIX), F32)
            prm = dict(
                mu=rwkv_mu[e].reshape(1, -1),
                vec=jnp.stack([rwkv_w0[e], rwkv_a0[e], rwkv_kk[e], rwkv_ka[e], rwkv_rk[e], gn_ret_g[e],
                               gn_wkv_g[e], jnp.zeros((W_MIX,), F32)]),
                w2p=jnp.concatenate([rwkv_w2[e], zpad], axis=0).astype(BF16),
                a2p=jnp.concatenate([zpad, rwkv_a2[e]], axis=0).astype(BF16),
                g2=rwkv_g2[e].astype(BF16))
            w_in = w_in_even[e].astype(BF16)
            w_out = w_out_even[e].astype(BF16)
            zeros = lambda *s: jnp.zeros(s, F32)
            o1p, o2p, wo, st_p = _even_layer(xp, zeros(bp, 1, B_SHIFT_WIDTH),
                                             zeros(bp, N_HEADS, HEAD_DIM, HEAD_DIM),
                                             zeros(bp, N_HEADS, HEAD_DIM, HEAD_DIM), w_in, w_out, prm, 256)
            o1s, o2s, _, st_s = _even_layer(xs, state_shift[e], state_ret[e], state_wkv[e], w_in, w_out,
                                            prm, ts)
            for n in range(3):
                p_even[n].append(st_p[n])
                s_even[n].append(st_s[n])
        else:
            o = l // 2
            w = w_in_odd[o]
            seg = {}
            off = 0
            for name, sz in (("qc", W_MIX), ("kc", HEAD_DIM), ("vc", HEAD_DIM), ("qi", IDX_HEADS * IDX_DIM),
                             ("ki", IDX_DIM), ("wi", IDX_HEADS), ("qd", W_MIX), ("kd", W_MIX), ("vd", W_MIX)):
                seg[name] = w[:, off:off + sz]
                off += sz
            w_main = jnp.concatenate([seg[n] for n in ("qc", "qi", "qd", "kd", "vd")], axis=1).astype(BF16)
            w_small = jnp.concatenate([seg["kc"], seg["vc"], seg["ki"], seg["wi"],
                                       jnp.zeros((d, 256 - 3 * 64 - IDX_HEADS), F32)], axis=1).astype(BF16)
            w_out = w_out_odd[o].astype(BF16)
            o1p, o2p, wo, rows_p = _odd_layer(xp, None, w_main, w_small, w_out, 128, 256, 512, 256)
            o1s, o2s, _, rows_s = _odd_layer(xs, (cache_dsa_k[o], cache_dsa_v[o], cache_idx_k[o],
                                                  cache_sb_k[o], cache_sb_v[o]),
                                             w_main, w_small, w_out, ts, ts, 512, 128)
            for n in range(5):
                p_odd[n].append(rows_p[n])
                s_odd[n].append(rows_s[n])

        xpf = _outproj_ln(o1p, o2p, wo, xp.reshape(bp * tp, d), ln1_g[l], ln1_b[l], 512)
        xsf = _outproj_ln(o1s, o2s, wo, xs.reshape(bs * ts, d), ln1_g[l], ln1_b[l], 512)
        wg, wu, wd = ffn_wg[l].astype(BF16), ffn_wu[l].astype(BF16), ffn_wd[l].astype(BF16)
        xp = _ffn_ln(xpf, wg, wu, wd, ln2_g[l], ln2_b[l], 1024, 256).reshape(bp, tp, d)
        xs = _ffn_ln(xsf, wg, wu, wd, ln2_g[l], ln2_b[l], 1024, 256).reshape(bs, ts, d)

    stack = lambda lists: [jnp.stack(a) for a in lists]
    return (xp, xs, *stack(p_even), *stack(p_odd), *stack(s_even), *stack(s_odd))
```

```python
import functools
import math

import jax
import jax.numpy as jnp
from jax import lax
from jax.experimental import pallas as pl
from jax.experimental.pallas import tpu as pltpu

F32 = jnp.float32
BF16 = jnp.bfloat16
I32 = jnp.int32

D_MODEL = 1024
HEAD_DIM = 64
CHUNK = 64
N_HEADS = 8
N_PAIRS = N_HEADS // 2
W_MIX = N_HEADS * HEAD_DIM
IDX_HEADS = 8
IDX_DIM = 64
TOPK_MAX = 256
DECAY_LORA = 64
AAA_LORA = 64
GATE_LORA = 128
B_SHIFT_WIDTH = 3 * W_MIX + DECAY_LORA + AAA_LORA + GATE_LORA
DEPTH = 4
D_FF = 2816
ALPHA = (2 * DEPTH) ** 0.25
LN_EPS = 1e-5
GN_EPS = 1e-5
WKV_GN_EPS = 64e-5
LANES = 128
NEG_BIG = -1e30
SB_EXIT_LOG = -104.0
LOG2E = 1.4426950408889634
VMEM_LIMIT = 56 * 1024 * 1024
INT_MIN = -2 ** 31


def _cparams(*sem):
    return pltpu.CompilerParams(dimension_semantics=sem, vmem_limit_bytes=VMEM_LIMIT)


def _dot(a, b):
    return jnp.dot(a.astype(BF16), b.astype(BF16), preferred_element_type=F32)


def _dot_nt(a, b):
    return lax.dot_general(a.astype(BF16), b.astype(BF16), (((1,), (1,)), ((), ())),
                           preferred_element_type=F32)


def _split_dot(x, m):
    hi = x.astype(BF16)
    lo = (x - hi.astype(F32)).astype(BF16)
    return (jnp.dot(hi, m, preferred_element_type=F32) + jnp.dot(lo, m, preferred_element_type=F32))


def _split_dot_left(m, x):
    hi = x.astype(BF16)
    lo = (x - hi.astype(F32)).astype(BF16)
    return (jnp.dot(m, hi, preferred_element_type=F32) + jnp.dot(m, lo, preferred_element_type=F32))


def _softplus(x):
    return jnp.maximum(x, 0.0) + jnp.log1p(jnp.exp(-jnp.abs(x)))


def _layernorm_rows(h, g, b):
    mu = jnp.mean(h, axis=-1, keepdims=True)
    d = h - mu
    var = jnp.mean(d * d, axis=-1, keepdims=True)
    return d * lax.rsqrt(var + LN_EPS) * g + b


def _proj_kernel(x_ref, w_ref, *o_refs):
    acc = jnp.dot(x_ref[...].astype(BF16), w_ref[...], preferred_element_type=F32)
    for o_ref in o_refs:
        o_ref[...] = acc.astype(o_ref.dtype)


def _proj(x, w, out_dtypes, tm, tn):
    m, k = x.shape
    n = w.shape[1]
    assert m % tm == 0 and n % tn == 0
    return pl.pallas_call(
        _proj_kernel,
        grid=(m // tm, n // tn),
        in_specs=[pl.BlockSpec((tm, k), lambda i, j: (i, 0)),
                  pl.BlockSpec((k, tn), lambda i, j: (0, j))],
        out_specs=[pl.BlockSpec((tm, tn), lambda i, j: (i, j)) for _ in out_dtypes],
        out_shape=[jax.ShapeDtypeStruct((m, n), dt) for dt in out_dtypes],
        compiler_params=_cparams("parallel", "arbitrary"),
    )(x, w)


def _outproj_ln_kernel(o1_ref, o2_ref, w1_ref, w2_ref, x_ref, g_ref, b_ref, y_ref):
    acc = jnp.dot(o1_ref[...], w1_ref[...], preferred_element_type=F32)
    acc = acc + jnp.dot(o2_ref[...], w2_ref[...], preferred_element_type=F32)
    y_ref[...] = _layernorm_rows(ALPHA * x_ref[...] + acc, g_ref[...], b_ref[...])


def _outproj_ln(o1, o2, w, x, g, b, tm):
    m, d = x.shape
    k1, k2 = o1.shape[1], o2.shape[1]
    row = lambda i: (i, 0)
    fix = lambda i: (0, 0)
    return pl.pallas_call(
        _outproj_ln_kernel,
        grid=(m // tm,),
        in_specs=[pl.BlockSpec((tm, k1), row), pl.BlockSpec((tm, k2), row),
                  pl.BlockSpec((k1, d), fix), pl.BlockSpec((k2, d), fix),
                  pl.BlockSpec((tm, d), row), pl.BlockSpec((1, d), fix), pl.BlockSpec((1, d), fix)],
        out_specs=pl.BlockSpec((tm, d), row),
        out_shape=jax.ShapeDtypeStruct((m, d), F32),
        compiler_params=_cparams("parallel"),
    )(o1, o2, w[:k1], w[k1:], x, g.reshape(1, d), b.reshape(1, d))


def _ffn_kernel(x_ref, wg_ref, wu_ref, wd_ref, g_ref, b_ref, y_ref, xb_s, acc_s):
    j = pl.program_id(1)

    @pl.when(j == 0)
    def _():
        xb_s[...] = x_ref[...].astype(BF16)
        acc_s[...] = jnp.zeros_like(acc_s)

    xb = xb_s[...]
    hg = jnp.dot(xb, wg_ref[...], preferred_element_type=F32)
    hu = jnp.dot(xb, wu_ref[...], preferred_element_type=F32)
    h = (hg * jax.nn.sigmoid(hg) * hu).astype(BF16)
    acc_s[...] += jnp.dot(h, wd_ref[...], preferred_element_type=F32)

    @pl.when(j == pl.num_programs(1) - 1)
    def _():
        y_ref[...] = _layernorm_rows(ALPHA * x_ref[...] + acc_s[...], g_ref[...], b_ref[...])


def _ffn_ln(x, wg, wu, wd, g, b, tm, tf):
    m, d = x.shape
    f = wg.shape[1]
    assert m % tm == 0 and f % tf == 0
    return pl.pallas_call(
        _ffn_kernel,
        grid=(m // tm, f // tf),
        in_specs=[pl.BlockSpec((tm, d), lambda i, j: (i, 0)),
                  pl.BlockSpec((d, tf), lambda i, j: (0, j)),
                  pl.BlockSpec((d, tf), lambda i, j: (0, j)),
                  pl.BlockSpec((tf, d), lambda i, j: (j, 0)),
                  pl.BlockSpec((1, d), lambda i, j: (0, 0)),
                  pl.BlockSpec((1, d), lambda i, j: (0, 0))],
        out_specs=pl.BlockSpec((tm, d), lambda i, j: (i, 0)),
        out_shape=jax.ShapeDtypeStruct((m, d), F32),
        scratch_shapes=[pltpu.VMEM((tm, d), BF16), pltpu.VMEM((tm, d), F32)],
        compiler_params=_cparams("parallel", "arbitrary"),
    )(x, wg, wu, wd, g.reshape(1, d), b.reshape(1, d))


def _even_kernel(pa_ref, pb_ref, shift0_ref, sret0_ref, swkv0_ref,
                 mu_ref, vec_ref, w2p_ref, a2p_ref, g2_ref, hsum_ref, lincl_ref, usuf_ref,
                 dmask_ref, qdec_ref, kdec_ref, cdec_ref,
                 oret_ref, ob_ref, sret_ref, swkv_ref, shift_ref,
                 sret_s, swkv_s, prev_s,
                 at_s, bt_s, kt_s, rt_s, bg_s, kg_s, v_s, gc_s, q_s, k_s, vr_s,
                 yret_s, ywkv_s, bonus_s, gate_s,
                 t_s, arb_s, wl_s, yl_s, sl_s, ol_s, rl_s):
    c = pl.program_id(1)
    tb = pb_ref.shape[1]
    n_chunks = tb // CHUNK

    @pl.when(c == 0)
    def _():
        sret_s[...] = sret0_ref[0]
        swkv_s[...] = swkv0_ref[0]
        prev_s[...] = shift0_ref[0]

    hsum = hsum_ref[...]

    def vec(i):
        return vec_ref[i:i + 1, :]

    w0, a0, k_k, k_a, r_k, g_ret, g_wkv = (vec(i) for i in range(7))

    pb = pb_ref[0]
    row = lax.broadcasted_iota(I32, (tb, 1), 0)
    prev = jnp.where(row == 0, prev_s[...], pltpu.roll(pb, 1, axis=0))
    prev_s[...] = pb[tb - 1:tb, :]
    mixed = pb + (prev - pb) * mu_ref[...]
    rb = mixed[:, 0:W_MIX]
    kb = mixed[:, W_MIX:2 * W_MIX]
    vb = mixed[:, 2 * W_MIX:3 * W_MIX]
    wa = mixed[:, 3 * W_MIX:3 * W_MIX + LANES]
    gl = mixed[:, 3 * W_MIX + LANES:3 * W_MIX + 2 * LANES]
    lane = lax.broadcasted_iota(I32, (1, LANES), 1)
    wa_t = jnp.where(lane < DECAY_LORA, jnp.tanh(wa), wa).astype(BF16)
    w_log = -_softplus(-(w0 + jnp.dot(wa_t, w2p_ref[...], preferred_element_type=F32))) - 0.5
    logw = -jnp.exp(w_log)
    lr = jax.nn.sigmoid(a0 + jnp.dot(wa_t, a2p_ref[...], preferred_element_type=F32))
    gate_s[...] = jnp.dot(jax.nn.sigmoid(gl).astype(BF16), g2_ref[...], preferred_element_type=F32)
    kkv = kb * k_k
    kkn = kkv * lax.rsqrt(jnp.maximum(_split_dot(kkv * kkv, hsum), 1e-12))
    k4 = kb * (1.0 + (lr - 1.0) * k_a)
    bonus_s[...] = _split_dot(rb * k4 * r_k, hsum) * vb
    cum = _split_dot_left(lincl_ref[...], logw)
    suf = _split_dot_left(usuf_ref[...], logw)
    ginv = jnp.exp(-cum)
    gsuf = jnp.exp(suf)
    at_s[...] = (-kkn * jnp.exp(cum - logw)).astype(BF16)
    bt_s[...] = (kkn * lr * ginv).astype(BF16)
    kt_s[...] = (k4 * ginv).astype(BF16)
    rt_s[...] = (rb * jnp.exp(cum)).astype(BF16)
    bg_s[...] = (kkn * lr * gsuf).astype(BF16)
    kg_s[...] = (k4 * gsuf).astype(BF16)
    v_s[...] = vb.astype(BF16)
    gc_s[...] = jnp.exp(cum + suf)

    q_s[...] = pa_ref[0, :, 0:W_MIX].astype(BF16)
    k_s[...] = (pa_ref[0, :, W_MIX:2 * W_MIX] * HEAD_DIM ** -0.5).astype(BF16)
    vr_s[...] = pa_ref[0, :, 2 * W_MIX:3 * W_MIX].astype(BF16)

    head0 = lane < HEAD_DIM
    r2 = lax.broadcasted_iota(I32, (2 * CHUNK, 2 * CHUNK), 0)
    c2 = lax.broadcasted_iota(I32, (2 * CHUNK, 2 * CHUNK), 1)
    same = (r2 // CHUNK) == (c2 // CHUNK)
    strict = same & (r2 > c2)
    incl = same & (r2 >= c2)

    eye = (r2 == c2).astype(F32)
    pairs = range(N_PAIRS)
    lanes_of = [slice(LANES * p, LANES * (p + 1)) for p in pairs]

    def stacked(ref, rows):
        out = []
        for p in pairs:
            x = ref[rows, lanes_of[p]]
            z = jnp.zeros_like(x)
            out.append(jnp.concatenate([jnp.where(head0, x, z), jnp.where(head0, z, x)], axis=0))
        return out

    def local_terms(ci, carry):
        rows = pl.ds(pl.multiple_of(ci * CHUNK, CHUNK), CHUNK)
        a2, b2, k2, rr2 = stacked(at_s, rows), stacked(bt_s, rows), stacked(kt_s, rows), stacked(rt_s, rows)
        v2, kg2 = stacked(v_s, rows), stacked(kg_s, rows)
        g = [_dot_nt(jnp.concatenate([a2[p], rr2[p]], axis=0), jnp.concatenate([b2[p], k2[p]], axis=0))
             for p in pairs]
        npow = [jnp.where(strict, g[p][0:128, 0:128], 0.0).astype(BF16) for p in pairs]
        a_ak = [jnp.where(strict, g[p][0:128, 128:256], 0.0) for p in pairs]
        a_rk = [jnp.where(incl, g[p][128:256, 128:256], 0.0) for p in pairs]
        for p in pairs:
            arb_s[ci * N_PAIRS + p] = jnp.where(incl, g[p][128:256, 0:128], 0.0).astype(BF16)
            wl_s[ci * N_PAIRS + p] = _dot(a_ak[p], v2[p])
            yl_s[ci * N_PAIRS + p] = _dot(a_rk[p], v2[p])
            sl_s[ci * N_PAIRS + p] = _dot(v2[p].astype(F32).T, kg2[p])
        tinv = [eye + npow[p].astype(F32) for p in pairs]
        for lvl in range(1, 6):
            npow = [_dot(npow[p], npow[p]).astype(BF16) for p in pairs]
            tinv = [tinv[p] + _dot(tinv[p], npow[p]) for p in pairs]
        for p in pairs:
            t_s[ci * N_PAIRS + p] = tinv[p].astype(BF16)
        q2, kr2, vr2 = stacked(q_s, rows), stacked(k_s, rows), stacked(vr_s, rows)
        att = [_dot_nt(q2[p], kr2[p]) * dmask_ref[p] for p in pairs]
        for p in pairs:
            ol_s[ci * N_PAIRS + p] = _dot(att[p], vr2[p])
            rl_s[ci * N_PAIRS + p] = _dot((kr2[p].astype(F32) * kdec_ref[p]).T, vr2[p])
        return carry

    lax.fori_loop(0, n_chunks, local_terms, 0)

    def recur(ci, carry):
        r0 = pl.multiple_of(ci * CHUNK, CHUNK)
        rows = pl.ds(r0, CHUNK)
        a2, rr2, bg2, q2 = stacked(at_s, rows), stacked(rt_s, rows), stacked(bg_s, rows), stacked(q_s, rows)
        s = [swkv_s[p] for p in pairs]
        sr = [sret_s[p] for p in pairs]
        ps = [_dot_nt(jnp.concatenate([a2[p], rr2[p]], axis=0), s[p]) for p in pairs]
        u = [_dot(t_s[ci * N_PAIRS + p], ps[p][0:128] + wl_s[ci * N_PAIRS + p]).astype(BF16) for p in pairs]
        for p in pairs:
            swkv_s[p] = (s[p] * gc_s[pl.ds(r0, 1), lanes_of[p]] + _dot(u[p].astype(F32).T, bg2[p])
                         + sl_s[ci * N_PAIRS + p])
        for p in pairs:
            y2 = ps[p][128:256] + _dot(arb_s[ci * N_PAIRS + p], u[p]) + yl_s[ci * N_PAIRS + p]
            ywkv_s[rows, lanes_of[p]] = y2[0:CHUNK] + y2[CHUNK:2 * CHUNK]
        for p in pairs:
            o2 = ol_s[ci * N_PAIRS + p] + _dot(q2[p], sr[p]) * qdec_ref[p]
            yret_s[rows, lanes_of[p]] = o2[0:CHUNK] + o2[CHUNK:2 * CHUNK]
            sret_s[p] = sr[p] * cdec_ref[p] + rl_s[ci * N_PAIRS + p]
        return carry

    lax.fori_loop(0, n_chunks, recur, 0)

    inv = 1.0 / HEAD_DIM
    yr = yret_s[...]
    d = yr - _split_dot(yr, hsum) * inv
    var = _split_dot(d * d, hsum) * inv
    ga = pa_ref[0, :, 3 * W_MIX:4 * W_MIX]
    oret_ref[0] = (d * lax.rsqrt(var + GN_EPS) * g_ret * (ga * jax.nn.sigmoid(ga))).astype(oret_ref.dtype)
    yw = ywkv_s[...]
    d = yw - _split_dot(yw, hsum) * inv
    var = _split_dot(d * d, hsum) * inv
    ob_ref[0] = ((d * lax.rsqrt(var + WKV_GN_EPS) * g_wkv + bonus_s[...]) * gate_s[...]).astype(ob_ref.dtype)

    @pl.when(c == pl.num_programs(1) - 1)
    def _():
        sret_ref[0] = sret_s[...]
        swkv_ref[0] = swkv_s[...]
        shift_ref[0] = prev_s[...]


def _pair_blockdiag(s):
    b = s.shape[0]
    s = s.reshape(b, N_PAIRS, 2, HEAD_DIM, HEAD_DIM)
    z = jnp.zeros_like(s[:, :, 0])
    top = jnp.concatenate([s[:, :, 0], z], axis=-1)
    bot = jnp.concatenate([z, s[:, :, 1]], axis=-1)
    return jnp.concatenate([top, bot], axis=-2)


def _pair_unblock(s):
    b = s.shape[0]
    h0 = s[:, :, 0:HEAD_DIM, 0:HEAD_DIM]
    h1 = s[:, :, HEAD_DIM:, HEAD_DIM:]
    return jnp.stack([h0, h1], axis=2).reshape(b, N_HEADS, HEAD_DIM, HEAD_DIM)


def _retention_constants():
    log_g = jnp.log1p(-jnp.exp2(-5.0 - jnp.arange(N_HEADS, dtype=F32)))
    lg = jnp.repeat(log_g.reshape(N_PAIRS, 2), CHUNK, axis=1)
    t = jnp.tile(jnp.arange(CHUNK, dtype=F32), 2)
    hd = jnp.repeat(jnp.arange(2), CHUNK)
    diff = t[:, None] - t[None, :]
    same = hd[:, None] == hd[None, :]
    dmask = jnp.where(same[None] & (diff[None] >= 0),
                      jnp.exp(lg[:, :, None] * jnp.maximum(diff, 0.0)[None]), 0.0)
    ones = jnp.ones((1, 1, LANES), F32)
    qdec = jnp.exp(lg * (t[None] + 1.0))[:, :, None] * ones
    kdec = jnp.exp(lg * (CHUNK - 1.0 - t[None]))[:, :, None] * ones
    cdec = jnp.exp(lg * CHUNK)[:, :, None] * ones
    return dmask, qdec, kdec, cdec


def _chunk_sum_matrices(tb):
    r = jnp.arange(tb)
    same = (r[:, None] // CHUNK) == (r[None, :] // CHUNK)
    lincl = (same & (r[None, :] <= r[:, None])).astype(BF16)
    usuf = (same & (r[None, :] > r[:, None])).astype(BF16)
    return lincl, usuf


def _even_mixer(pa, pb, shift0, sret0, swkv0, prm, tb):
    b, t, _ = pa.shape
    assert t % tb == 0 and tb % CHUNK == 0
    lincl, usuf = _chunk_sum_matrices(tb)
    dmask, qdec, kdec, cdec = _retention_constants()
    hd = jnp.arange(W_MIX) // HEAD_DIM
    hsum = (hd[:, None] == hd[None, :]).astype(BF16)

    def fixed(shape):
        zero = (0,) * len(shape)
        return pl.BlockSpec(shape, lambda i, c: zero)

    def per_batch(shape):
        zero = (0,) * len(shape)
        return pl.BlockSpec((1,) + shape, lambda i, c: (i,) + zero)

    def per_block(w):
        return pl.BlockSpec((1, tb, w), lambda i, c: (i, c, 0))

    pair_state = (N_PAIRS, LANES, LANES)
    wide = lambda dt: pltpu.VMEM((tb, W_MIX), dt)
    per_chunk = lambda dt: pltpu.VMEM((tb // CHUNK * N_PAIRS, LANES, LANES), dt)
    outs = pl.pallas_call(
        _even_kernel,
        grid=(b, t // tb),
        in_specs=[per_block(4 * W_MIX), per_block(B_SHIFT_WIDTH),
                  per_batch((1, B_SHIFT_WIDTH)), per_batch(pair_state), per_batch(pair_state),
                  fixed((1, B_SHIFT_WIDTH)), fixed((8, W_MIX)),
                  fixed((LANES, W_MIX)), fixed((LANES, W_MIX)), fixed((GATE_LORA, W_MIX)),
                  fixed((W_MIX, W_MIX)), fixed((tb, tb)), fixed((tb, tb)),
                  fixed(pair_state), fixed(pair_state), fixed(pair_state), fixed(pair_state)],
        out_specs=[per_block(W_MIX), per_block(W_MIX),
                   per_batch(pair_state), per_batch(pair_state), per_batch((1, B_SHIFT_WIDTH))],
        out_shape=[jax.ShapeDtypeStruct((b, t, W_MIX), BF16), jax.ShapeDtypeStruct((b, t, W_MIX), BF16),
                   jax.ShapeDtypeStruct((b,) + pair_state, F32), jax.ShapeDtypeStruct((b,) + pair_state, F32),
                   jax.ShapeDtypeStruct((b, 1, B_SHIFT_WIDTH), F32)],
        scratch_shapes=[pltpu.VMEM(pair_state, F32), pltpu.VMEM(pair_state, F32),
                        pltpu.VMEM((1, B_SHIFT_WIDTH), F32)]
                       + [wide(BF16)] * 7 + [wide(F32)] + [wide(BF16)] * 3 + [wide(F32)] * 4
                       + [per_chunk(BF16)] * 2 + [per_chunk(F32)] * 5,
        compiler_params=_cparams("parallel", "arbitrary"),
    )(pa, pb, shift0, _pair_blockdiag(sret0), _pair_blockdiag(swkv0),
      prm["mu"], prm["vec"], prm["w2p"], prm["a2p"], prm["g2"], hsum, lincl, usuf,
      dmask, qdec, kdec, cdec)
    o_ret, o_b, sret, swkv, shift = outs
    return o_ret, o_b, shift, _pair_unblock(sret), _pair_unblock(swkv)


def _dsa_kernel(qc_ref, qi_ref, wi_ref, kc_ref, va_ref, ki_ref, slope_ref, o_ref,
                key_s, m_s, acc_s, *, tq, tk, tka, past, topk):
    i = pl.program_id(1)
    q0 = past + i * tq
    n_kb = (q0 + tq + tk - 1) // tk

    qpos = q0 + lax.broadcasted_iota(I32, (tq, 1), 0)
    qchunk = qpos // CHUNK
    lane_k = lax.broadcasted_iota(I32, (1, tk), 1)

    wi = wi_ref[0] * (IDX_DIM ** -0.5 * IDX_HEADS ** -0.5)
    qi = qi_ref[0]
    qi_h = [qi[:, IDX_DIM * h:IDX_DIM * (h + 1)] for h in range(IDX_HEADS)]
    wi_h = [wi[:, h:h + 1] for h in range(IDX_HEADS)]

    def score_body(kb, carry):
        k0 = pl.multiple_of(kb * tk, tk)
        ki = ki_ref[0, pl.ds(k0, tk), :]
        s = jnp.zeros((tq, tk), F32)
        for h in range(IDX_HEADS):
            s = s + jnp.maximum(_dot_nt(qi_h[h], ki), 0.0) * wi_h[h]
        s = s + 0.0
        adm = ((k0 + lane_k) // CHUNK) <= qchunk
        s = jnp.where(adm, s, -jnp.inf)
        bits = pltpu.bitcast(s, I32)
        key_s[:, pl.ds(k0, tk)] = bits ^ ((bits >> 31) & 0x7FFFFFFF)
        return carry

    lax.fori_loop(0, n_kb, score_body, 0)

    def count(pred):
        def body(kb, acc):
            k0 = pl.multiple_of(kb * tk, tk)
            hit = jnp.where(pred(key_s[:, pl.ds(k0, tk)], k0 + lane_k), 1.0, 0.0)
            for g in range(tk // LANES):
                acc = acc + hit[:, LANES * g:LANES * (g + 1)]
            return acc
        acc = lax.fori_loop(0, n_kb, body, jnp.zeros((tq, LANES), F32))
        return jnp.sum(acc, axis=1, keepdims=True)

    base = jnp.where(count(lambda key, kp: key >= 0) >= topk, 0, INT_MIN).astype(I32)

    def bit_body(b, base):
        cand = base | (jnp.int32(1) << (30 - b))
        return jnp.where(count(lambda key, kp: key >= cand) >= topk, cand, base)

    thr = lax.fori_loop(0, 31, bit_body, base)
    need = topk - count(lambda key, kp: key > thr)
    n_eq = count(lambda key, kp: key == thr)

    def tie_cut(_):
        def tbit(b, m):
            cand = m | (jnp.int32(1) << (14 - b))
            return jnp.where(count(lambda key, kp: (key == thr) & (kp < cand)) < need, cand, m)
        return lax.fori_loop(0, 15, tbit, jnp.zeros((tq, 1), I32))

    last_eq = lax.cond(jnp.max(n_eq - need) > 0.0, tie_cut,
                       lambda _: jnp.full((tq, 1), 2 ** 30, I32), 0)

    qc = qc_ref[0]
    qc_h = [qc[:, HEAD_DIM * h:HEAD_DIM * (h + 1)] for h in range(N_HEADS)]
    m_s[...] = jnp.full(m_s.shape, NEG_BIG, F32)
    acc_s[...] = jnp.zeros(acc_s.shape, F32)
    lane_a = lax.broadcasted_iota(I32, (1, tka), 1)
    n_kba = (q0 + tq + tka - 1) // tka

    def att_body(kb, carry):
        k0 = pl.multiple_of(kb * tka, tka)
        kpos = k0 + lane_a
        key = key_s[:, pl.ds(k0, tka)]
        sel = ((kpos // CHUNK) <= qchunk) & ((key > thr) | ((key == thr) & (kpos <= last_eq)))
        ndist = -jnp.abs(qpos - kpos).astype(F32)
        kc = kc_ref[0, pl.ds(k0, tka), :]
        va = va_ref[0, pl.ds(k0, tka), :]
        heads = range(N_HEADS)
        dots = [_dot_nt(qc_h[h], kc) for h in heads]
        logit = [jnp.where(sel, dots[h] + slope_ref[h] * ndist, NEG_BIG) for h in heads]
        m_old = [m_s[h] for h in heads]
        m_new = [jnp.maximum(m_old[h], jnp.max(logit[h], axis=1, keepdims=True)) for h in heads]
        p = [jnp.exp2(logit[h] - m_new[h]).astype(BF16) for h in heads]
        pv = [jnp.dot(p[h], va, preferred_element_type=F32) for h in heads]
        for h in heads:
            acc_s[h] = jnp.exp2(m_old[h] - m_new[h]) * acc_s[h] + pv[h]
            m_s[h] = m_new[h]
        return carry

    lax.fori_loop(0, n_kba, att_body, 0)
    outs = []
    for h in range(N_HEADS):
        acc = acc_s[h]
        outs.append(acc[:, 0:HEAD_DIM] / acc[:, HEAD_DIM:HEAD_DIM + 1])
    o_ref[0] = jnp.concatenate(outs, axis=1).astype(o_ref.dtype)


def _dsa(pq, wi, kc, va, ki, past, tq, tk, tka):
    b, t, _ = pq.shape
    lp = kc.shape[1]
    n_keys = past + t
    topk = min(TOPK_MAX, n_keys // 4)
    assert t % tq == 0 and tq % CHUNK == 0 and lp % tk == 0 and tk % tka == 0 and lp >= n_keys
    slopes = jnp.exp2(-8.0 * (jnp.arange(N_HEADS, dtype=F32) + 1.0) / N_HEADS) * LOG2E
    qblk = lambda w, col: pl.BlockSpec((1, tq, w), lambda bi, i: (bi, i, col))
    kblk = lambda w: pl.BlockSpec((1, lp, w), lambda bi, i: (bi, 0, 0))
    return pl.pallas_call(
        functools.partial(_dsa_kernel, tq=tq, tk=tk, tka=tka, past=past, topk=topk),
        grid=(b, t // tq),
        in_specs=[qblk(W_MIX, 0), qblk(W_MIX, 1), qblk(IDX_HEADS, 0), kblk(HEAD_DIM), kblk(LANES),
                  kblk(IDX_DIM), pl.BlockSpec(memory_space=pltpu.SMEM)],
        out_specs=qblk(W_MIX, 0),
        out_shape=jax.ShapeDtypeStruct((b, t, W_MIX), BF16),
        scratch_shapes=[pltpu.VMEM((tq, lp), I32), pltpu.VMEM((N_HEADS, tq, 1), F32),
                        pltpu.VMEM((N_HEADS, tq, LANES), F32)],
        compiler_params=_cparams("parallel", "arbitrary"),
    )(pq, pq, wi, kc, va, ki, slopes)


def _sb_kernel(q_ref, k_ref, v_ref, o_ref, *, tq, tk, past):
    i = pl.program_id(2)
    q0 = past + i * tq
    qpos = q0 + lax.broadcasted_iota(I32, (tq, 1), 0)
    lane_k = lax.broadcasted_iota(I32, (1, tk), 1)
    lane = lax.broadcasted_iota(I32, (1, LANES), 1)
    head0 = lane < HEAD_DIM
    rk = lax.broadcasted_iota(I32, (tk, tk), 0)
    ck = lax.broadcasted_iota(I32, (tk, tk), 1)
    later = (rk > ck).astype(BF16)

    q = q_ref[0]
    zq = jnp.zeros_like(q)
    q_heads = (jnp.where(head0, q, zq), jnp.where(head0, zq, q))

    j_hi = (q0 + tq - 2) // tk
    n_full = q0 // tk

    def blocks(js, carry):
        heads = range(2)
        k0 = [pl.multiple_of(j * tk, tk) for j, _ in js]
        kblk = [k_ref[0, pl.ds(k, tk), :] for k in k0]
        vblk = [v_ref[0, pl.ds(k, tk), :] for k in k0]
        z = [[_dot_nt(q_heads[h], kb) for h in heads] for kb in kblk]
        lrem, lsig, causal = [], [], []
        for n, (_, masked) in enumerate(js):
            causal.append((k0[n] + lane_k) < qpos if masked else None)
            row_r, row_s = [], []
            for h in heads:
                r = -(jnp.maximum(z[n][h], 0.0) + jnp.log(1.0 + jnp.exp(-jnp.abs(z[n][h]))))
                row_s.append(z[n][h] + r)
                row_r.append(jnp.where(causal[n], r, 0.0) if masked else r)
            lrem.append(row_r)
            lsig.append(row_s)
        run = [carry[h][0] for h in heads]
        acc = [carry[h][1] for h in heads]
        run_in = []
        for n in range(len(js)):
            run_in.append(list(run))
            run = [run[h] + jnp.sum(lrem[n][h], axis=1, keepdims=True) for h in heads]
        after = [[jnp.dot(lrem[n][h].astype(BF16), later, preferred_element_type=F32) for h in heads]
                 for n in range(len(js))]
        for n, (_, masked) in enumerate(js):
            for h in heads:
                att = jnp.exp(lsig[n][h] + after[n][h] + run_in[n][h])
                if masked:
                    att = jnp.where(causal[n], att, 0.0)
                acc[h] = acc[h] + _dot(att, vblk[n])
        return tuple((run[h], acc[h]) for h in heads)

    init = tuple((jnp.zeros((tq, 1), F32), jnp.zeros((tq, LANES), F32)) for _ in range(2))
    carry = lax.cond(n_full > 0,
                     lambda: blocks([(j_hi, True), (j_hi - 1, False)], init),
                     lambda: blocks([(j_hi, True)], init))

    def more(state):
        s, c = state
        return (s < n_full) & (jnp.max(jnp.maximum(c[0][0], c[1][0])) > SB_EXIT_LOG)

    _, carry = lax.while_loop(more, lambda st: (st[0] + 1, blocks([(n_full - 1 - st[0], False)], st[1])),
                              (jnp.int32(1), carry))
    o_ref[0] = jnp.where(head0, carry[0][1], carry[1][1]).astype(o_ref.dtype)


def _stickbreak(q, q_col, k, v, kv_cols, past, tq, tk):
    b, t, _ = q.shape
    lp = k.shape[1]
    assert t % tq == 0 and lp % tk == 0 and lp >= past + t
    assert tk % tq == 0 and past % tq == 0
    k_col, v_col = kv_cols
    return pl.pallas_call(
        functools.partial(_sb_kernel, tq=tq, tk=tk, past=past),
        grid=(b, N_PAIRS, t // tq),
        in_specs=[pl.BlockSpec((1, tq, LANES), lambda bi, p, i: (bi, i, q_col + p)),
                  pl.BlockSpec((1, lp, LANES), lambda bi, p, i: (bi, 0, k_col + p)),
                  pl.BlockSpec((1, lp, LANES), lambda bi, p, i: (bi, 0, v_col + p))],
        out_specs=pl.BlockSpec((1, tq, LANES), lambda bi, p, i: (bi, i, p)),
        out_shape=jax.ShapeDtypeStruct((b, t, W_MIX), BF16),
        compiler_params=_cparams("parallel", "parallel", "arbitrary"),
    )(q, k, v)


def _row_tile(m, cap):
    t = cap
    while m % t:
        t //= 2
    return t


def _even_layer(x, shift0, sret0, swkv0, w_in, w_out, prm, tb):
    b, t, d = x.shape
    xf = x.reshape(b * t, d)
    tm = _row_tile(b * t, 1024)
    (pa,) = _proj(xf, w_in[:, :4 * W_MIX], (F32,), tm, 2 * W_MIX)
    (pb,) = _proj(xf, w_in[:, 4 * W_MIX:], (F32,), tm, B_SHIFT_WIDTH // 2)
    o_ret, o_b, shift, sret, swkv = _even_mixer(pa.reshape(b, t, -1), pb.reshape(b, t, -1),
                                                shift0, sret0, swkv0, prm, tb)
    return o_ret.reshape(b * t, -1), o_b.reshape(b * t, -1), w_out, (shift, sret, swkv)


def _pad_rows(a, lp):
    return jnp.pad(a, ((0, 0), (0, lp - a.shape[1]), (0, 0)))


def _odd_layer(x, caches, w_q, w_kv, w_small, w_out, tq_dsa, tq_sb, tk_dsa, tka_dsa, tk_sb):
    b, t, d = x.shape
    xf = x.reshape(b * t, d)
    tm = _row_tile(b * t, 1024)
    (pq16,) = _proj(xf, w_q, (BF16,), tm, w_q.shape[1] // 2)
    pkv32, pkv16 = _proj(xf, w_kv, (F32, BF16), tm, w_kv.shape[1] // 2)
    (ps32,) = _proj(xf, w_small, (F32,), tm, w_small.shape[1])
    pq16 = pq16.reshape(b, t, -1)
    pkv32 = pkv32.reshape(b, t, -1)
    pkv16 = pkv16.reshape(b, t, -1)
    ps32 = ps32.reshape(b, t, -1)
    kc, vc, ki = (ps32[..., 64 * n:64 * (n + 1)] for n in range(3))
    wi = ps32[..., 192:192 + IDX_HEADS]
    new_rows = (kc, vc, ki, pkv32[..., :W_MIX].reshape(b, t, N_HEADS, HEAD_DIM),
                pkv32[..., W_MIX:].reshape(b, t, N_HEADS, HEAD_DIM))
    small = [a.astype(BF16) for a in (kc, vc, ki)]
    if caches is None:
        past = 0
        k_sb = v_sb = pkv16
        sb_cols = (0, N_PAIRS)
    else:
        past = caches[0].shape[1]
        small = [jnp.concatenate([c.astype(BF16), n], axis=1) for c, n in zip(caches[:3], small)]
        lp_sb = -(-(past + t) // tk_sb) * tk_sb
        k_sb, v_sb = (_pad_rows(jnp.concatenate([c.reshape(b, past, -1).astype(BF16), n], axis=1), lp_sb)
                      for c, n in zip(caches[3:], (pkv16[..., :W_MIX], pkv16[..., W_MIX:])))
        sb_cols = (0, 0)
    n_keys = past + t
    lp_dsa = -(-n_keys // tk_dsa) * tk_dsa
    kc_a, vc_a, ki_a = (_pad_rows(a, lp_dsa) for a in small)
    one = jnp.ones((b, lp_dsa, 1), BF16)
    va = jnp.concatenate([vc_a, one, jnp.zeros((b, lp_dsa, LANES - HEAD_DIM - 1), BF16)], axis=-1)
    oc = _dsa(pq16, wi, kc_a, va, ki_a, past, tq_dsa, tk_dsa, tka_dsa)
    od = _stickbreak(pq16, 2 * N_PAIRS, k_sb, v_sb, sb_cols, past, tq_sb, tk_sb)
    return oc.reshape(b * t, -1), od.reshape(b * t, -1), w_out, new_rows


def kernel(x_prompt, x_sample, state_shift, state_ret, state_wkv, cache_dsa_k, cache_dsa_v, cache_idx_k,
           cache_sb_k, cache_sb_v, w_in_even, w_out_even, gn_ret_g, rwkv_mu, rwkv_w0, rwkv_w2, rwkv_a0,
           rwkv_a2, rwkv_g2, rwkv_kk, rwkv_ka, rwkv_rk, gn_wkv_g, w_in_odd, w_out_odd, ln1_g, ln1_b,
           ln2_g, ln2_b, ffn_wg, ffn_wu, ffn_wd):
    xp, xs = x_prompt, x_sample
    bp, tp, d = xp.shape
    bs, ts, _ = xs.shape
    p_even, s_even = [[], [], []], [[], [], []]
    p_odd, s_odd = [[] for _ in range(5)], [[] for _ in range(5)]

    for l in range(DEPTH):
        if l % 2 == 0:
            e = l // 2
            zpad = jnp.zeros((DECAY_LORA, W_MIX), F32)
            prm = dict(
                mu=rwkv_mu[e].reshape(1, -1),
                vec=jnp.stack([rwkv_w0[e], rwkv_a0[e], rwkv_kk[e], rwkv_ka[e], rwkv_rk[e], gn_ret_g[e],
                               gn_wkv_g[e], jnp.zeros((W_MIX,), F32)]),
                w2p=jnp.concatenate([rwkv_w2[e], zpad], axis=0).astype(BF16),
                a2p=jnp.concatenate([zpad, rwkv_a2[e]], axis=0).astype(BF16),
                g2=rwkv_g2[e].astype(BF16))
            w_in = w_in_even[e].astype(BF16)
            w_out = w_out_even[e].astype(BF16)
            zeros = lambda *s: jnp.zeros(s, F32)
            o1p, o2p, wo, st_p = _even_layer(xp, zeros(bp, 1, B_SHIFT_WIDTH),
                                             zeros(bp, N_HEADS, HEAD_DIM, HEAD_DIM),
                                             zeros(bp, N_HEADS, HEAD_DIM, HEAD_DIM), w_in, w_out, prm, 256)
            o1s, o2s, _, st_s = _even_layer(xs, state_shift[e], state_ret[e], state_wkv[e], w_in, w_out,
                                            prm, ts)
            for n in range(3):
                p_even[n].append(st_p[n])
                s_even[n].append(st_s[n])
        else:
            o = l // 2
            w = w_in_odd[o]
            seg = {}
            off = 0
            for name, sz in (("qc", W_MIX), ("kc", HEAD_DIM), ("vc", HEAD_DIM), ("qi", IDX_HEADS * IDX_DIM),
                             ("ki", IDX_DIM), ("wi", IDX_HEADS), ("qd", W_MIX), ("kd", W_MIX), ("vd", W_MIX)):
                seg[name] = w[:, off:off + sz]
                off += sz
            q_scale = HEAD_DIM ** -0.5
            w_q = jnp.concatenate([seg["qc"] * (q_scale * LOG2E), seg["qi"], seg["qd"] * q_scale],
                                  axis=1).astype(BF16)
            w_kv = jnp.concatenate([seg["kd"], seg["vd"]], axis=1).astype(BF16)
            w_small = jnp.concatenate([seg["kc"], seg["vc"], seg["ki"], seg["wi"],
                                       jnp.zeros((d, 256 - 3 * 64 - IDX_HEADS), F32)], axis=1).astype(BF16)
            w_out = w_out_odd[o].astype(BF16)
            o1p, o2p, wo, rows_p = _odd_layer(xp, None, w_q, w_kv, w_small, w_out, 128, 128, 512, 256, 256)
            o1s, o2s, _, rows_s = _odd_layer(xs, (cache_dsa_k[o], cache_dsa_v[o], cache_idx_k[o],
                                                  cache_sb_k[o], cache_sb_v[o]),
                                             w_q, w_kv, w_small, w_out, ts, ts, 512, 256, 256)
            for n in range(5):
                p_odd[n].append(rows_p[n])
                s_odd[n].append(rows_s[n])

        xpf = _outproj_ln(o1p, o2p, wo, xp.reshape(bp * tp, d), ln1_g[l], ln1_b[l], 512)
        xsf = _outproj_ln(o1s, o2s, wo, xs.reshape(bs * ts, d), ln1_g[l], ln1_b[l], 512)
        wg, wu, wd = ffn_wg[l].astype(BF16), ffn_wu[l].astype(BF16), ffn_wd[l].astype(BF16)
        xp = _ffn_ln(xpf, wg, wu, wd, ln2_g[l], ln2_b[l], 1024, 256).reshape(bp, tp, d)
        xs = _ffn_ln(xsf, wg, wu, wd, ln2_g[l], ln2_b[l], 1024, 256).reshape(bs, ts, d)

    stack = lambda lists: [jnp.stack(a) for a in lists]
    return (xp, xs, *stack(p_even), *stack(p_odd), *stack(s_even), *stack(s_odd))
```

```python
import functools
import math

import jax
import jax.numpy as jnp
from jax import lax
from jax.experimental import pallas as pl
from jax.experimental.pallas import tpu as pltpu

F32 = jnp.float32
BF16 = jnp.bfloat16
I32 = jnp.int32

D_MODEL = 1024
HEAD_DIM = 64
CHUNK = 64
LOG2_CHUNK = 6
N_HEADS = 8
N_PAIRS = N_HEADS // 2
W_MIX = N_HEADS * HEAD_DIM
IDX_HEADS = 8
IDX_DIM = 64
TOPK_MAX = 256
DECAY_LORA = 64
AAA_LORA = 64
GATE_LORA = 128
B_SHIFT_WIDTH = 3 * W_MIX + DECAY_LORA + AAA_LORA + GATE_LORA
DEPTH = 4
D_FF = 2816
ALPHA = (2 * DEPTH) ** 0.25
LN_EPS = 1e-5
GN_EPS = 1e-5
WKV_GN_EPS = 64e-5
LANES = 128
NEG_BIG = -1e30
SB_EXIT_LOG = -104.0
LOG2E = 1.4426950408889634
VMEM_LIMIT = 56 * 1024 * 1024
INT_MIN = -2 ** 31


def _cparams(*sem):
    return pltpu.CompilerParams(dimension_semantics=sem, vmem_limit_bytes=VMEM_LIMIT)


def _dot(a, b):
    return jnp.dot(a.astype(BF16), b.astype(BF16), preferred_element_type=F32)


def _dot_nt(a, b):
    return lax.dot_general(a.astype(BF16), b.astype(BF16), (((1,), (1,)), ((), ())),
                           preferred_element_type=F32)


def _split_dot(x, m):
    hi = x.astype(BF16)
    lo = (x - hi.astype(F32)).astype(BF16)
    return (jnp.dot(hi, m, preferred_element_type=F32) + jnp.dot(lo, m, preferred_element_type=F32))


def _split_dot_left(m, x):
    hi = x.astype(BF16)
    lo = (x - hi.astype(F32)).astype(BF16)
    return (jnp.dot(m, hi, preferred_element_type=F32) + jnp.dot(m, lo, preferred_element_type=F32))


def _softplus(x):
    return jnp.maximum(x, 0.0) + jnp.log1p(jnp.exp(-jnp.abs(x)))


def _layernorm_rows(h, g, b):
    mu = jnp.mean(h, axis=-1, keepdims=True)
    d = h - mu
    var = jnp.mean(d * d, axis=-1, keepdims=True)
    return d * lax.rsqrt(var + LN_EPS) * g + b


def _proj_kernel(x_ref, w_ref, *o_refs):
    acc = jnp.dot(x_ref[...].astype(BF16), w_ref[...], preferred_element_type=F32)
    for o_ref in o_refs:
        o_ref[...] = acc.astype(o_ref.dtype)


def _proj(x, w, out_dtypes, tm, tn):
    m, k = x.shape
    n = w.shape[1]
    assert m % tm == 0 and n % tn == 0
    return pl.pallas_call(
        _proj_kernel,
        grid=(m // tm, n // tn),
        in_specs=[pl.BlockSpec((tm, k), lambda i, j: (i, 0)),
                  pl.BlockSpec((k, tn), lambda i, j: (0, j))],
        out_specs=[pl.BlockSpec((tm, tn), lambda i, j: (i, j)) for _ in out_dtypes],
        out_shape=[jax.ShapeDtypeStruct((m, n), dt) for dt in out_dtypes],
        compiler_params=_cparams("parallel", "arbitrary"),
    )(x, w)


def _split_bf16(a):
    hi = a.astype(BF16)
    return hi, (a - hi.astype(F32)).astype(BF16)


def _proj3_kernel(x_ref, wh_ref, wl_ref, o1_ref, o2_ref):
    xh, xl = _split_bf16(x_ref[...])
    acc = jnp.dot(xh, wh_ref[...], preferred_element_type=F32)
    acc = acc + jnp.dot(xh, wl_ref[...], preferred_element_type=F32)
    acc = acc + jnp.dot(xl, wh_ref[...], preferred_element_type=F32)
    n1 = o1_ref.shape[1]
    o1_ref[...] = acc[:, :n1]
    o2_ref[...] = acc[:, n1:]


def _proj3(x, w, n1, tm):
    m, k = x.shape
    n = w.shape[1]
    wh, wl = _split_bf16(w)
    return pl.pallas_call(
        _proj3_kernel,
        grid=(m // tm,),
        in_specs=[pl.BlockSpec((tm, k), lambda i: (i, 0)),
                  pl.BlockSpec((k, n), lambda i: (0, 0)), pl.BlockSpec((k, n), lambda i: (0, 0))],
        out_specs=[pl.BlockSpec((tm, n1), lambda i: (i, 0)), pl.BlockSpec((tm, n - n1), lambda i: (i, 0))],
        out_shape=[jax.ShapeDtypeStruct((m, n1), F32), jax.ShapeDtypeStruct((m, n - n1), F32)],
        compiler_params=_cparams("parallel"),
    )(x, wh, wl)


def _outproj_ln_kernel(o1_ref, o2_ref, w1_ref, w2_ref, x_ref, g_ref, b_ref, y_ref):
    acc = jnp.dot(o1_ref[...], w1_ref[...], preferred_element_type=F32)
    acc = acc + jnp.dot(o2_ref[...], w2_ref[...], preferred_element_type=F32)
    y_ref[...] = _layernorm_rows(ALPHA * x_ref[...] + acc, g_ref[...], b_ref[...])


def _outproj_ln(o1, o2, w, x, g, b, tm):
    m, d = x.shape
    k1, k2 = o1.shape[1], o2.shape[1]
    row = lambda i: (i, 0)
    fix = lambda i: (0, 0)
    return pl.pallas_call(
        _outproj_ln_kernel,
        grid=(m // tm,),
        in_specs=[pl.BlockSpec((tm, k1), row), pl.BlockSpec((tm, k2), row),
                  pl.BlockSpec((k1, d), fix), pl.BlockSpec((k2, d), fix),
                  pl.BlockSpec((tm, d), row), pl.BlockSpec((1, d), fix), pl.BlockSpec((1, d), fix)],
        out_specs=pl.BlockSpec((tm, d), row),
        out_shape=jax.ShapeDtypeStruct((m, d), F32),
        compiler_params=_cparams("parallel"),
    )(o1, o2, w[:k1], w[k1:], x, g.reshape(1, d), b.reshape(1, d))


def _ffn_kernel(x_ref, wg_ref, wu_ref, wd_ref, g_ref, b_ref, y_ref, xb_s, acc_s):
    j = pl.program_id(1)

    @pl.when(j == 0)
    def _():
        xb_s[...] = x_ref[...].astype(BF16)
        acc_s[...] = jnp.zeros_like(acc_s)

    xb = xb_s[...]
    hg = jnp.dot(xb, wg_ref[...], preferred_element_type=F32)
    hu = jnp.dot(xb, wu_ref[...], preferred_element_type=F32)
    h = (hg * jax.nn.sigmoid(hg) * hu).astype(BF16)
    acc_s[...] += jnp.dot(h, wd_ref[...], preferred_element_type=F32)

    @pl.when(j == pl.num_programs(1) - 1)
    def _():
        y_ref[...] = _layernorm_rows(ALPHA * x_ref[...] + acc_s[...], g_ref[...], b_ref[...])


def _ffn_ln(x, wg, wu, wd, g, b, tm, tf):
    m, d = x.shape
    f = wg.shape[1]
    assert m % tm == 0 and f % tf == 0
    return pl.pallas_call(
        _ffn_kernel,
        grid=(m // tm, f // tf),
        in_specs=[pl.BlockSpec((tm, d), lambda i, j: (i, 0)),
                  pl.BlockSpec((d, tf), lambda i, j: (0, j)),
                  pl.BlockSpec((d, tf), lambda i, j: (0, j)),
                  pl.BlockSpec((tf, d), lambda i, j: (j, 0)),
                  pl.BlockSpec((1, d), lambda i, j: (0, 0)),
                  pl.BlockSpec((1, d), lambda i, j: (0, 0))],
        out_specs=pl.BlockSpec((tm, d), lambda i, j: (i, 0)),
        out_shape=jax.ShapeDtypeStruct((m, d), F32),
        scratch_shapes=[pltpu.VMEM((tm, d), BF16), pltpu.VMEM((tm, d), F32)],
        compiler_params=_cparams("parallel", "arbitrary"),
    )(x, wg, wu, wd, g.reshape(1, d), b.reshape(1, d))


def _even_kernel(pa_ref, pb_ref, shift0_ref, sret0_ref, swkv0_ref,
                 mu_ref, vec_ref, w2p_ref, a2p_ref, g2_ref, hsum_ref, lincl_ref, usuf_ref,
                 dmask_ref, qdec_ref, kdec_ref, cdec_ref,
                 oret_ref, ob_ref, sret_ref, swkv_ref, shift_ref,
                 sret_s, swkv_s, prev_s,
                 at_s, bt_s, kt_s, rt_s, bg_s, kg_s, v_s, gc_s, q_s, k_s, vr_s,
                 yret_s, ywkv_s, bonus_s, gate_s,
                 t_s, arb_s, wl_s, yl_s, sl_s, ol_s, rl_s):
    c = pl.program_id(1)
    tb = pb_ref.shape[1]
    n_chunks = tb // CHUNK

    @pl.when(c == 0)
    def _():
        sret_s[...] = sret0_ref[0]
        swkv_s[...] = swkv0_ref[0]
        prev_s[...] = shift0_ref[0]

    hsum = hsum_ref[...]

    def vec(i):
        return vec_ref[i:i + 1, :]

    w0, a0, k_k, k_a, r_k, g_ret, g_wkv = (vec(i) for i in range(7))

    pb = pb_ref[0]
    row = lax.broadcasted_iota(I32, (tb, 1), 0)
    prev = jnp.where(row == 0, prev_s[...], pltpu.roll(pb, 1, axis=0))
    prev_s[...] = pb[tb - 1:tb, :]
    mixed = pb + (prev - pb) * mu_ref[...]
    rb = mixed[:, 0:W_MIX]
    kb = mixed[:, W_MIX:2 * W_MIX]
    vb = mixed[:, 2 * W_MIX:3 * W_MIX]
    wa = mixed[:, 3 * W_MIX:3 * W_MIX + LANES]
    gl = mixed[:, 3 * W_MIX + LANES:3 * W_MIX + 2 * LANES]
    lane = lax.broadcasted_iota(I32, (1, LANES), 1)
    wa_t = jnp.where(lane < DECAY_LORA, jnp.tanh(wa), wa).astype(BF16)
    w_log = -_softplus(-(w0 + jnp.dot(wa_t, w2p_ref[...], preferred_element_type=F32))) - 0.5
    logw = -jnp.exp(w_log)
    lr = jax.nn.sigmoid(a0 + jnp.dot(wa_t, a2p_ref[...], preferred_element_type=F32))
    gate_s[...] = jnp.dot(jax.nn.sigmoid(gl).astype(BF16), g2_ref[...], preferred_element_type=F32)
    kkv = kb * k_k
    kkn = kkv * lax.rsqrt(jnp.maximum(_split_dot(kkv * kkv, hsum), 1e-12))
    k4 = kb * (1.0 + (lr - 1.0) * k_a)
    bonus_s[...] = _split_dot(rb * k4 * r_k, hsum) * vb
    cum = _split_dot_left(lincl_ref[...], logw)
    suf = _split_dot_left(usuf_ref[...], logw)
    ginv = jnp.exp(-cum)
    gsuf = jnp.exp(suf)
    at_s[...] = (-kkn * jnp.exp(cum - logw)).astype(BF16)
    bt_s[...] = (kkn * lr * ginv).astype(BF16)
    kt_s[...] = (k4 * ginv).astype(BF16)
    rt_s[...] = (rb * jnp.exp(cum)).astype(BF16)
    bg_s[...] = (kkn * lr * gsuf).astype(BF16)
    kg_s[...] = (k4 * gsuf).astype(BF16)
    v_s[...] = vb.astype(BF16)
    gc_s[...] = jnp.exp(cum + suf)

    q_s[...] = pa_ref[0, :, 0:W_MIX].astype(BF16)
    k_s[...] = (pa_ref[0, :, W_MIX:2 * W_MIX] * HEAD_DIM ** -0.5).astype(BF16)
    vr_s[...] = pa_ref[0, :, 2 * W_MIX:3 * W_MIX].astype(BF16)

    head0 = lane < HEAD_DIM
    r2 = lax.broadcasted_iota(I32, (2 * CHUNK, 2 * CHUNK), 0)
    c2 = lax.broadcasted_iota(I32, (2 * CHUNK, 2 * CHUNK), 1)
    same = (r2 // CHUNK) == (c2 // CHUNK)
    strict = same & (r2 > c2)
    incl = same & (r2 >= c2)

    eye = (r2 == c2).astype(F32)
    pairs = range(N_PAIRS)
    lanes_of = [slice(LANES * p, LANES * (p + 1)) for p in pairs]

    def stacked(ref, rows):
        out = []
        for p in pairs:
            x = ref[rows, lanes_of[p]]
            z = jnp.zeros_like(x)
            out.append(jnp.concatenate([jnp.where(head0, x, z), jnp.where(head0, z, x)], axis=0))
        return out

    def local_terms(ci, carry):
        rows = pl.ds(pl.multiple_of(ci * CHUNK, CHUNK), CHUNK)
        a2, b2, k2, rr2 = stacked(at_s, rows), stacked(bt_s, rows), stacked(kt_s, rows), stacked(rt_s, rows)
        v2, kg2 = stacked(v_s, rows), stacked(kg_s, rows)
        g = [_dot_nt(jnp.concatenate([a2[p], rr2[p]], axis=0), jnp.concatenate([b2[p], k2[p]], axis=0))
             for p in pairs]
        npow = [jnp.where(strict, g[p][0:128, 0:128], 0.0).astype(BF16) for p in pairs]
        a_ak = [jnp.where(strict, g[p][0:128, 128:256], 0.0) for p in pairs]
        a_rk = [jnp.where(incl, g[p][128:256, 128:256], 0.0) for p in pairs]
        for p in pairs:
            arb_s[ci * N_PAIRS + p] = jnp.where(incl, g[p][128:256, 0:128], 0.0).astype(BF16)
            wl_s[ci * N_PAIRS + p] = _dot(a_ak[p], v2[p])
            yl_s[ci * N_PAIRS + p] = _dot(a_rk[p], v2[p])
            sl_s[ci * N_PAIRS + p] = _dot(v2[p].astype(F32).T, kg2[p])
        tinv = [eye + npow[p].astype(F32) for p in pairs]
        for lvl in range(1, 6):
            npow = [_dot(npow[p], npow[p]).astype(BF16) for p in pairs]
            tinv = [tinv[p] + _dot(tinv[p], npow[p]) for p in pairs]
        for p in pairs:
            t_s[ci * N_PAIRS + p] = tinv[p].astype(BF16)
        q2, kr2, vr2 = stacked(q_s, rows), stacked(k_s, rows), stacked(vr_s, rows)
        att = [_dot_nt(q2[p], kr2[p]) * dmask_ref[p] for p in pairs]
        for p in pairs:
            ol_s[ci * N_PAIRS + p] = _dot(att[p], vr2[p])
            rl_s[ci * N_PAIRS + p] = _dot((kr2[p].astype(F32) * kdec_ref[p]).T, vr2[p])
        return carry

    lax.fori_loop(0, n_chunks, local_terms, 0)

    def recur(ci, carry):
        r0 = pl.multiple_of(ci * CHUNK, CHUNK)
        rows = pl.ds(r0, CHUNK)
        a2, rr2, bg2, q2 = stacked(at_s, rows), stacked(rt_s, rows), stacked(bg_s, rows), stacked(q_s, rows)
        s = [swkv_s[p] for p in pairs]
        sr = [sret_s[p] for p in pairs]
        ps = [_dot_nt(jnp.concatenate([a2[p], rr2[p]], axis=0), s[p]) for p in pairs]
        u = [_dot(t_s[ci * N_PAIRS + p], ps[p][0:128] + wl_s[ci * N_PAIRS + p]).astype(BF16) for p in pairs]
        for p in pairs:
            swkv_s[p] = (s[p] * gc_s[pl.ds(r0, 1), lanes_of[p]] + _dot(u[p].astype(F32).T, bg2[p])
                         + sl_s[ci * N_PAIRS + p])
        for p in pairs:
            y2 = ps[p][128:256] + _dot(arb_s[ci * N_PAIRS + p], u[p]) + yl_s[ci * N_PAIRS + p]
            ywkv_s[rows, lanes_of[p]] = y2[0:CHUNK] + y2[CHUNK:2 * CHUNK]
        for p in pairs:
            o2 = ol_s[ci * N_PAIRS + p] + _dot(q2[p], sr[p]) * qdec_ref[p]
            yret_s[rows, lanes_of[p]] = o2[0:CHUNK] + o2[CHUNK:2 * CHUNK]
            sret_s[p] = sr[p] * cdec_ref[p] + rl_s[ci * N_PAIRS + p]
        return carry

    lax.fori_loop(0, n_chunks, recur, 0)

    inv = 1.0 / HEAD_DIM
    yr = yret_s[...]
    d = yr - _split_dot(yr, hsum) * inv
    var = _split_dot(d * d, hsum) * inv
    ga = pa_ref[0, :, 3 * W_MIX:4 * W_MIX]
    oret_ref[0] = (d * lax.rsqrt(var + GN_EPS) * g_ret * (ga * jax.nn.sigmoid(ga))).astype(oret_ref.dtype)
    yw = ywkv_s[...]
    d = yw - _split_dot(yw, hsum) * inv
    var = _split_dot(d * d, hsum) * inv
    ob_ref[0] = ((d * lax.rsqrt(var + WKV_GN_EPS) * g_wkv + bonus_s[...]) * gate_s[...]).astype(ob_ref.dtype)

    @pl.when(c == pl.num_programs(1) - 1)
    def _():
        sret_ref[0] = sret_s[...]
        swkv_ref[0] = swkv_s[...]
        shift_ref[0] = prev_s[...]


def _pair_blockdiag(s):
    b = s.shape[0]
    s = s.reshape(b, N_PAIRS, 2, HEAD_DIM, HEAD_DIM)
    z = jnp.zeros_like(s[:, :, 0])
    top = jnp.concatenate([s[:, :, 0], z], axis=-1)
    bot = jnp.concatenate([z, s[:, :, 1]], axis=-1)
    return jnp.concatenate([top, bot], axis=-2)


def _pair_unblock(s):
    b = s.shape[0]
    h0 = s[:, :, 0:HEAD_DIM, 0:HEAD_DIM]
    h1 = s[:, :, HEAD_DIM:, HEAD_DIM:]
    return jnp.stack([h0, h1], axis=2).reshape(b, N_HEADS, HEAD_DIM, HEAD_DIM)


def _retention_constants():
    log_g = jnp.log1p(-jnp.exp2(-5.0 - jnp.arange(N_HEADS, dtype=F32)))
    lg = jnp.repeat(log_g.reshape(N_PAIRS, 2), CHUNK, axis=1)
    t = jnp.tile(jnp.arange(CHUNK, dtype=F32), 2)
    hd = jnp.repeat(jnp.arange(2), CHUNK)
    diff = t[:, None] - t[None, :]
    same = hd[:, None] == hd[None, :]
    dmask = jnp.where(same[None] & (diff[None] >= 0),
                      jnp.exp(lg[:, :, None] * jnp.maximum(diff, 0.0)[None]), 0.0)
    ones = jnp.ones((1, 1, LANES), F32)
    qdec = jnp.exp(lg * (t[None] + 1.0))[:, :, None] * ones
    kdec = jnp.exp(lg * (CHUNK - 1.0 - t[None]))[:, :, None] * ones
    cdec = jnp.exp(lg * CHUNK)[:, :, None] * ones
    return dmask, qdec, kdec, cdec


def _chunk_sum_matrices(tb):
    r = jnp.arange(tb)
    same = (r[:, None] // CHUNK) == (r[None, :] // CHUNK)
    lincl = (same & (r[None, :] <= r[:, None])).astype(BF16)
    usuf = (same & (r[None, :] > r[:, None])).astype(BF16)
    return lincl, usuf


def _even_mixer(pa, pb, shift0, sret0, swkv0, prm, tb):
    b, t, _ = pa.shape
    assert t % tb == 0 and tb % CHUNK == 0
    lincl, usuf = _chunk_sum_matrices(tb)
    dmask, qdec, kdec, cdec = _retention_constants()
    hd = jnp.arange(W_MIX) // HEAD_DIM
    hsum = (hd[:, None] == hd[None, :]).astype(BF16)

    def fixed(shape):
        zero = (0,) * len(shape)
        return pl.BlockSpec(shape, lambda i, c: zero)

    def per_batch(shape):
        zero = (0,) * len(shape)
        return pl.BlockSpec((1,) + shape, lambda i, c: (i,) + zero)

    def per_block(w):
        return pl.BlockSpec((1, tb, w), lambda i, c: (i, c, 0))

    pair_state = (N_PAIRS, LANES, LANES)
    wide = lambda dt: pltpu.VMEM((tb, W_MIX), dt)
    per_chunk = lambda dt: pltpu.VMEM((tb // CHUNK * N_PAIRS, LANES, LANES), dt)
    outs = pl.pallas_call(
        _even_kernel,
        grid=(b, t // tb),
        in_specs=[per_block(4 * W_MIX), per_block(B_SHIFT_WIDTH),
                  per_batch((1, B_SHIFT_WIDTH)), per_batch(pair_state), per_batch(pair_state),
                  fixed((1, B_SHIFT_WIDTH)), fixed((8, W_MIX)),
                  fixed((LANES, W_MIX)), fixed((LANES, W_MIX)), fixed((GATE_LORA, W_MIX)),
                  fixed((W_MIX, W_MIX)), fixed((tb, tb)), fixed((tb, tb)),
                  fixed(pair_state), fixed(pair_state), fixed(pair_state), fixed(pair_state)],
        out_specs=[per_block(W_MIX), per_block(W_MIX),
                   per_batch(pair_state), per_batch(pair_state), per_batch((1, B_SHIFT_WIDTH))],
        out_shape=[jax.ShapeDtypeStruct((b, t, W_MIX), BF16), jax.ShapeDtypeStruct((b, t, W_MIX), BF16),
                   jax.ShapeDtypeStruct((b,) + pair_state, F32), jax.ShapeDtypeStruct((b,) + pair_state, F32),
                   jax.ShapeDtypeStruct((b, 1, B_SHIFT_WIDTH), F32)],
        scratch_shapes=[pltpu.VMEM(pair_state, F32), pltpu.VMEM(pair_state, F32),
                        pltpu.VMEM((1, B_SHIFT_WIDTH), F32)]
                       + [wide(BF16)] * 7 + [wide(F32)] + [wide(BF16)] * 3 + [wide(F32)] * 4
                       + [per_chunk(BF16)] * 2 + [per_chunk(F32)] * 5,
        compiler_params=_cparams("parallel", "arbitrary"),
    )(pa, pb, shift0, _pair_blockdiag(sret0), _pair_blockdiag(swkv0),
      prm["mu"], prm["vec"], prm["w2p"], prm["a2p"], prm["g2"], hsum, lincl, usuf,
      dmask, qdec, kdec, cdec)
    o_ret, o_b, sret, swkv, shift = outs
    return o_ret, o_b, shift, _pair_unblock(sret), _pair_unblock(swkv)


SUBLANES = 8


def _dsa_kernel(qc_ref, qi_ref, wit_ref, kc_ref, vat_ref, ki_ref, slope_ref, o_ref,
                key_s, qt_s, q3_s, acc_s, *, tq, tk, tka, past, topk):
    i = pl.program_id(1)
    q0 = past + i * tq
    n_kb = (q0 + tq + tk - 1) // tk

    qpos = q0 + lax.broadcasted_iota(I32, (1, tq), 1)
    qchunk = qpos >> LOG2_CHUNK
    sub_k = lax.broadcasted_iota(I32, (tk, 1), 0)
    sub_k_chunk = sub_k >> LOG2_CHUNK

    qt_s[...] = qc_ref[0].astype(F32).T.astype(BF16)
    qit_hi, qit_lo = _split_bf16(qi_ref[0].T)
    zero_rows = jnp.zeros((IDX_DIM, tq), BF16)
    for h in range(IDX_HEADS):
        rows = slice(IDX_DIM * h, IDX_DIM * (h + 1))
        q3_s[4 * IDX_DIM * h:4 * IDX_DIM * (h + 1), :] = jnp.concatenate(
            [qit_hi[rows], qit_lo[rows], qit_hi[rows], zero_rows], axis=0)

    wit = wit_ref[0] * (IDX_DIM ** -0.5 * IDX_HEADS ** -0.5)

    def score_body(kb, carry):
        k0 = pl.multiple_of(kb * tk, tk)
        ki = ki_ref[0, pl.ds(k0, tk), :]
        s = jnp.zeros((tk, tq), F32)
        for h in range(IDX_HEADS):
            qit = q3_s[4 * IDX_DIM * h:4 * IDX_DIM * (h + 1), :]
            s = s + jnp.maximum(jnp.dot(ki, qit, preferred_element_type=F32), 0.0) * wit[h:h + 1, :]
        s = s + 0.0
        adm = sub_k_chunk <= (qchunk - k0 // CHUNK)
        s = jnp.where(adm, s, -jnp.inf)
        bits = pltpu.bitcast(s, I32)
        key_s[pl.ds(k0, tk), :] = bits ^ ((bits >> 31) & 0x7FFFFFFF)
        return carry

    lax.fori_loop(0, n_kb, score_body, 0)

    def count(pred):
        def body(kb, acc):
            k0 = pl.multiple_of(kb * tk, tk)
            hit = jnp.where(pred(key_s[pl.ds(k0, tk), :], k0 + sub_k), 1.0, 0.0)
            parts = [hit[SUBLANES * g:SUBLANES * (g + 1), :] for g in range(tk // SUBLANES)]
            while len(parts) > 1:
                parts = [parts[n] + parts[n + 1] for n in range(0, len(parts), 2)]
            return acc + parts[0]
        acc = lax.fori_loop(0, n_kb, body, jnp.zeros((SUBLANES, tq), F32))
        return jnp.sum(acc, axis=0, keepdims=True)

    base = jnp.where(count(lambda key, kp: key >= 0) >= topk, 0, INT_MIN).astype(I32)

    def bit_body(b, base):
        cand = base | (jnp.int32(1) << (30 - b))
        return jnp.where(count(lambda key, kp: key >= cand) >= topk, cand, base)

    thr = lax.fori_loop(0, 31, bit_body, base)
    need = topk - count(lambda key, kp: key > thr)
    n_eq = count(lambda key, kp: key == thr)

    def tie_cut(_):
        def tbit(b, m):
            cand = m | (jnp.int32(1) << (14 - b))
            return jnp.where(count(lambda key, kp: (key == thr) & (kp < cand)) < need, cand, m)
        return lax.fori_loop(0, 15, tbit, jnp.zeros((1, tq), I32))

    last_eq = lax.cond(jnp.max(n_eq - need) > 0.0, tie_cut,
                       lambda _: jnp.full((1, tq), 2 ** 30, I32), 0)

    acc_s[...] = jnp.zeros(acc_s.shape, F32)
    sub_a = lax.broadcasted_iota(I32, (tka, 1), 0)
    sub_a_chunk = sub_a >> LOG2_CHUNK
    sub_af = sub_a.astype(F32)
    qposf = qpos.astype(F32)
    n_kba = (q0 + tq + tka - 1) // tka
    heads = range(N_HEADS)

    def att_body(kb, m):
        k0 = pl.multiple_of(kb * tka, tka)
        kpos = k0 + sub_a
        key = key_s[pl.ds(k0, tka), :]
        sel = ((sub_a_chunk <= (qchunk - k0 // CHUNK))
               & ((key > thr) | ((key == thr) & (kpos <= last_eq))))
        ndist = -jnp.abs(qposf - (sub_af + k0.astype(F32)))
        kc = kc_ref[0, pl.ds(k0, tka), :]
        vat = vat_ref[0, :, pl.ds(k0, tka)]
        dots = [jnp.dot(kc, qt_s[HEAD_DIM * h:HEAD_DIM * (h + 1), :], preferred_element_type=F32)
                for h in heads]
        logit = [jnp.where(sel, dots[h] + slope_ref[h] * ndist, NEG_BIG) for h in heads]
        m_old = [m[h:h + 1, :] for h in heads]
        m_new = [jnp.maximum(m_old[h], jnp.max(logit[h], axis=0, keepdims=True)) for h in heads]
        p = [jnp.exp2(logit[h] - m_new[h]).astype(BF16) for h in heads]
        pv = [jnp.dot(vat, p[h], preferred_element_type=F32) for h in heads]
        for h in heads:
            acc_s[h] = jnp.exp2(m_old[h] - m_new[h]) * acc_s[h] + pv[h]
        return jnp.concatenate(m_new, axis=0)

    lax.fori_loop(0, n_kba, att_body, jnp.full((N_HEADS, tq), NEG_BIG, F32))
    outs = []
    for h in heads:
        acc = acc_s[h]
        outs.append(acc[0:HEAD_DIM, :] / acc[HEAD_DIM:HEAD_DIM + 1, :])
    o_ref[0] = jnp.concatenate(outs, axis=0).T.astype(o_ref.dtype)


def _dsa(pq, qi, wi, kc, vc, ki, past, tq, tk, tka):
    b, t, _ = pq.shape
    lp = kc.shape[1]
    ki_hi, ki_lo = _split_bf16(ki)
    ki3 = jnp.concatenate([ki_hi, ki_hi, ki_lo, jnp.zeros_like(ki_hi)], axis=-1)
    n_keys = past + t
    topk = min(TOPK_MAX, n_keys // 4)
    assert t % tq == 0 and tq % CHUNK == 0 and lp % tk == 0 and tk % tka == 0 and lp >= n_keys
    slopes = jnp.exp2(-8.0 * (jnp.arange(N_HEADS, dtype=F32) + 1.0) / N_HEADS) * LOG2E
    vat = jnp.concatenate([jnp.swapaxes(vc, 1, 2), jnp.ones((b, 1, lp), BF16),
                           jnp.zeros((b, LANES - HEAD_DIM - 1, lp), BF16)], axis=1)
    wit = jnp.swapaxes(wi, 1, 2)
    qblk = lambda w, col: pl.BlockSpec((1, tq, w), lambda bi, i: (bi, i, col))
    kblk = lambda w: pl.BlockSpec((1, lp, w), lambda bi, i: (bi, 0, 0))
    return pl.pallas_call(
        functools.partial(_dsa_kernel, tq=tq, tk=tk, tka=tka, past=past, topk=topk),
        grid=(b, t // tq),
        in_specs=[qblk(W_MIX, 0), qblk(W_MIX, 0),
                  pl.BlockSpec((1, IDX_HEADS, tq), lambda bi, i: (bi, 0, i)),
                  kblk(HEAD_DIM), pl.BlockSpec((1, LANES, lp), lambda bi, i: (bi, 0, 0)),
                  kblk(4 * IDX_DIM), pl.BlockSpec(memory_space=pltpu.SMEM)],
        out_specs=qblk(W_MIX, 0),
        out_shape=jax.ShapeDtypeStruct((b, t, W_MIX), BF16),
        scratch_shapes=[pltpu.VMEM((lp, tq), I32), pltpu.VMEM((W_MIX, tq), BF16),
                        pltpu.VMEM((4 * IDX_DIM * IDX_HEADS, tq), BF16),
                        pltpu.VMEM((N_HEADS, LANES, tq), F32)],
        compiler_params=_cparams("parallel", "arbitrary"),
    )(pq, qi, wit, kc, vat, ki3, slopes)


def _sb_kernel(q_ref, k_ref, v_ref, o_ref, *, tq, tk, past):
    i = pl.program_id(2)
    q0 = past + i * tq
    qpos = q0 + lax.broadcasted_iota(I32, (tq, 1), 0)
    lane_k = lax.broadcasted_iota(I32, (1, tk), 1)
    lane = lax.broadcasted_iota(I32, (1, LANES), 1)
    head0 = lane < HEAD_DIM
    rk = lax.broadcasted_iota(I32, (tk, tk), 0)
    ck = lax.broadcasted_iota(I32, (tk, tk), 1)
    later = (rk > ck).astype(BF16)

    q = q_ref[0]
    zq = jnp.zeros_like(q)
    q_heads = (jnp.where(head0, q, zq), jnp.where(head0, zq, q))

    j_hi = (q0 + tq - 2) // tk
    n_full = q0 // tk

    def blocks(js, carry):
        heads = range(2)
        k0 = [pl.multiple_of(j * tk, tk) for j, _ in js]
        kblk = [k_ref[0, pl.ds(k, tk), :] for k in k0]
        vblk = [v_ref[0, pl.ds(k, tk), :] for k in k0]
        z = [[_dot_nt(q_heads[h], kb) for h in heads] for kb in kblk]
        lrem, lsig, causal = [], [], []
        for n, (_, masked) in enumerate(js):
            causal.append((k0[n] + lane_k) < qpos if masked else None)
            row_r, row_s = [], []
            for h in heads:
                r = -(jnp.maximum(z[n][h], 0.0) + jnp.log(1.0 + jnp.exp(-jnp.abs(z[n][h]))))
                row_s.append(z[n][h] + r)
                row_r.append(jnp.where(causal[n], r, 0.0) if masked else r)
            lrem.append(row_r)
            lsig.append(row_s)
        run = [carry[h][0] for h in heads]
        acc = [carry[h][1] for h in heads]
        run_in = []
        for n in range(len(js)):
            run_in.append(list(run))
            run = [run[h] + jnp.sum(lrem[n][h], axis=1, keepdims=True) for h in heads]
        after = [[jnp.dot(lrem[n][h].astype(BF16), later, preferred_element_type=F32) for h in heads]
                 for n in range(len(js))]
        for n, (_, masked) in enumerate(js):
            for h in heads:
                att = jnp.exp(lsig[n][h] + after[n][h] + run_in[n][h])
                if masked:
                    att = jnp.where(causal[n], att, 0.0)
                acc[h] = acc[h] + _dot(att, vblk[n])
        return tuple((run[h], acc[h]) for h in heads)

    init = tuple((jnp.zeros((tq, 1), F32), jnp.zeros((tq, LANES), F32)) for _ in range(2))
    carry = lax.cond(n_full > 0,
                     lambda: blocks([(j_hi, True), (j_hi - 1, False)], init),
                     lambda: blocks([(j_hi, True)], init))

    def more(state):
        s, c = state
        return (s < n_full) & (jnp.max(jnp.maximum(c[0][0], c[1][0])) > SB_EXIT_LOG)

    _, carry = lax.while_loop(more, lambda st: (st[0] + 1, blocks([(n_full - 1 - st[0], False)], st[1])),
                              (jnp.int32(1), carry))
    o_ref[0] = jnp.where(head0, carry[0][1], carry[1][1]).astype(o_ref.dtype)


def _stickbreak(q, q_col, k, v, kv_cols, past, tq, tk):
    b, t, _ = q.shape
    lp = k.shape[1]
    assert t % tq == 0 and lp % tk == 0 and lp >= past + t
    assert tk % tq == 0 and past % tq == 0
    k_col, v_col = kv_cols
    return pl.pallas_call(
        functools.partial(_sb_kernel, tq=tq, tk=tk, past=past),
        grid=(b, N_PAIRS, t // tq),
        in_specs=[pl.BlockSpec((1, tq, LANES), lambda bi, p, i: (bi, i, q_col + p)),
                  pl.BlockSpec((1, lp, LANES), lambda bi, p, i: (bi, 0, k_col + p)),
                  pl.BlockSpec((1, lp, LANES), lambda bi, p, i: (bi, 0, v_col + p))],
        out_specs=pl.BlockSpec((1, tq, LANES), lambda bi, p, i: (bi, i, p)),
        out_shape=jax.ShapeDtypeStruct((b, t, W_MIX), BF16),
        compiler_params=_cparams("parallel", "parallel", "arbitrary"),
    )(q, k, v)


def _row_tile(m, cap):
    t = cap
    while m % t:
        t //= 2
    return t


def _even_layer(x, shift0, sret0, swkv0, w_in, w_out, prm, tb):
    b, t, d = x.shape
    xf = x.reshape(b * t, d)
    tm = _row_tile(b * t, 1024)
    (pa,) = _proj(xf, w_in[:, :4 * W_MIX], (F32,), tm, 2 * W_MIX)
    (pb,) = _proj(xf, w_in[:, 4 * W_MIX:], (F32,), tm, B_SHIFT_WIDTH // 2)
    o_ret, o_b, shift, sret, swkv = _even_mixer(pa.reshape(b, t, -1), pb.reshape(b, t, -1),
                                                shift0, sret0, swkv0, prm, tb)
    return o_ret.reshape(b * t, -1), o_b.reshape(b * t, -1), w_out, (shift, sret, swkv)


def _pad_rows(a, lp):
    return jnp.pad(a, ((0, 0), (0, lp - a.shape[1]), (0, 0)))


def _odd_layer(x, caches, w_q, w_kv, w_small, w_sel, w_out, tq_dsa, tq_sb, tk_dsa, tka_dsa, tk_sb):
    b, t, d = x.shape
    xf = x.reshape(b * t, d)
    tm = _row_tile(b * t, 1024)
    (pq16,) = _proj(xf, w_q, (BF16,), tm, w_q.shape[1] // 2)
    pkv32, pkv16 = _proj(xf, w_kv, (F32, BF16), tm, w_kv.shape[1] // 2)
    (ps32,) = _proj(xf, w_small, (F32,), tm, w_small.shape[1])
    qi, sel_small = _proj3(xf, w_sel, IDX_HEADS * IDX_DIM, tm)
    pq16 = pq16.reshape(b, t, -1)
    pkv32 = pkv32.reshape(b, t, -1)
    pkv16 = pkv16.reshape(b, t, -1)
    ps32 = ps32.reshape(b, t, -1)
    sel_small = sel_small.reshape(b, t, -1)
    kc, vc = ps32[..., :HEAD_DIM], ps32[..., HEAD_DIM:]
    ki = sel_small[..., :IDX_DIM]
    wi = sel_small[..., IDX_DIM:IDX_DIM + IDX_HEADS]
    new_rows = (kc, vc, ki, pkv32[..., :W_MIX].reshape(b, t, N_HEADS, HEAD_DIM),
                pkv32[..., W_MIX:].reshape(b, t, N_HEADS, HEAD_DIM))
    kc_a, vc_a, ki_a = kc.astype(BF16), vc.astype(BF16), ki
    if caches is None:
        past = 0
        k_sb = v_sb = pkv16
        sb_cols = (0, N_PAIRS)
    else:
        past = caches[0].shape[1]
        kc_a = jnp.concatenate([caches[0].astype(BF16), kc_a], axis=1)
        vc_a = jnp.concatenate([caches[1].astype(BF16), vc_a], axis=1)
        ki_a = jnp.concatenate([caches[2], ki_a], axis=1)
        lp_sb = -(-(past + t) // tk_sb) * tk_sb
        k_sb, v_sb = (_pad_rows(jnp.concatenate([c.reshape(b, past, -1).astype(BF16), n], axis=1), lp_sb)
                      for c, n in zip(caches[3:], (pkv16[..., :W_MIX], pkv16[..., W_MIX:])))
        sb_cols = (0, 0)
    n_keys = past + t
    lp_dsa = -(-n_keys // tk_dsa) * tk_dsa
    kc_a, vc_a, ki_a = (_pad_rows(a, lp_dsa) for a in (kc_a, vc_a, ki_a))
    oc = _dsa(pq16, qi.reshape(b, t, -1), wi, kc_a, vc_a, ki_a, past, tq_dsa, tk_dsa, tka_dsa)
    od = _stickbreak(pq16, N_PAIRS, k_sb, v_sb, sb_cols, past, tq_sb, tk_sb)
    return oc.reshape(b * t, -1), od.reshape(b * t, -1), w_out, new_rows


def kernel(x_prompt, x_sample, state_shift, state_ret, state_wkv, cache_dsa_k, cache_dsa_v, cache_idx_k,
           cache_sb_k, cache_sb_v, w_in_even, w_out_even, gn_ret_g, rwkv_mu, rwkv_w0, rwkv_w2, rwkv_a0,
           rwkv_a2, rwkv_g2, rwkv_kk, rwkv_ka, rwkv_rk, gn_wkv_g, w_in_odd, w_out_odd, ln1_g, ln1_b,
           ln2_g, ln2_b, ffn_wg, ffn_wu, ffn_wd):
    xp, xs = x_prompt, x_sample
    bp, tp, d = xp.shape
    bs, ts, _ = xs.shape
    p_even, s_even = [[], [], []], [[], [], []]
    p_odd, s_odd = [[] for _ in range(5)], [[] for _ in range(5)]

    for l in range(DEPTH):
        if l % 2 == 0:
            e = l // 2
            zpad = jnp.zeros((DECAY_LORA, W_MIX), F32)
            prm = dict(
                mu=rwkv_mu[e].reshape(1, -1),
                vec=jnp.stack([rwkv_w0[e], rwkv_a0[e], rwkv_kk[e], rwkv_ka[e], rwkv_rk[e], gn_ret_g[e],
                               gn_wkv_g[e], jnp.zeros((W_MIX,), F32)]),
                w2p=jnp.concatenate([rwkv_w2[e], zpad], axis=0).astype(BF16),
                a2p=jnp.concatenate([zpad, rwkv_a2[e]], axis=0).astype(BF16),
                g2=rwkv_g2[e].astype(BF16))
            w_in = w_in_even[e].astype(BF16)
            w_out = w_out_even[e].astype(BF16)
            zeros = lambda *s: jnp.zeros(s, F32)
            o1p, o2p, wo, st_p = _even_layer(xp, zeros(bp, 1, B_SHIFT_WIDTH),
                                             zeros(bp, N_HEADS, HEAD_DIM, HEAD_DIM),
                                             zeros(bp, N_HEADS, HEAD_DIM, HEAD_DIM), w_in, w_out, prm, 256)
            o1s, o2s, _, st_s = _even_layer(xs, state_shift[e], state_ret[e], state_wkv[e], w_in, w_out,
                                            prm, ts)
            for n in range(3):
                p_even[n].append(st_p[n])
                s_even[n].append(st_s[n])
        else:
            o = l // 2
            w = w_in_odd[o]
            seg = {}
            off = 0
            for name, sz in (("qc", W_MIX), ("kc", HEAD_DIM), ("vc", HEAD_DIM), ("qi", IDX_HEADS * IDX_DIM),
                             ("ki", IDX_DIM), ("wi", IDX_HEADS), ("qd", W_MIX), ("kd", W_MIX), ("vd", W_MIX)):
                seg[name] = w[:, off:off + sz]
                off += sz
            q_scale = HEAD_DIM ** -0.5
            w_q = jnp.concatenate([seg["qc"] * (q_scale * LOG2E), seg["qd"] * q_scale], axis=1).astype(BF16)
            w_kv = jnp.concatenate([seg["kd"], seg["vd"]], axis=1).astype(BF16)
            w_small = jnp.concatenate([seg["kc"], seg["vc"]], axis=1).astype(BF16)
            w_sel = jnp.concatenate([seg["qi"], seg["ki"], seg["wi"],
                                     jnp.zeros((d, LANES - IDX_DIM - IDX_HEADS), F32)], axis=1)
            w_out = w_out_odd[o].astype(BF16)
            o1p, o2p, wo, rows_p = _odd_layer(xp, None, w_q, w_kv, w_small, w_sel, w_out,
                                              128, 128, 512, 512, 256)
            o1s, o2s, _, rows_s = _odd_layer(xs, (cache_dsa_k[o], cache_dsa_v[o], cache_idx_k[o],
                                                  cache_sb_k[o], cache_sb_v[o]),
                                             w_q, w_kv, w_small, w_sel, w_out, ts, ts, 512, 256, 256)
            for n in range(5):
                p_odd[n].append(rows_p[n])
                s_odd[n].append(rows_s[n])

        xpf = _outproj_ln(o1p, o2p, wo, xp.reshape(bp * tp, d), ln1_g[l], ln1_b[l], 512)
        xsf = _outproj_ln(o1s, o2s, wo, xs.reshape(bs * ts, d), ln1_g[l], ln1_b[l], 512)
        wg, wu, wd = ffn_wg[l].astype(BF16), ffn_wu[l].astype(BF16), ffn_wd[l].astype(BF16)
        xp = _ffn_ln(xpf, wg, wu, wd, ln2_g[l], ln2_b[l], 1024, 256).reshape(bp, tp, d)
        xs = _ffn_ln(xsf, wg, wu, wd, ln2_g[l], ln2_b[l], 1024, 256).reshape(bs, ts, d)

    stack = lambda lists: [jnp.stack(a) for a in lists]
    return (xp, xs, *stack(p_even), *stack(p_odd), *stack(s_even), *stack(s_odd))
```

```python
import functools
import math

import jax
import jax.numpy as jnp
from jax import lax
from jax.experimental import pallas as pl
from jax.experimental.pallas import tpu as pltpu

F32 = jnp.float32
BF16 = jnp.bfloat16
I32 = jnp.int32

D_MODEL = 1024
HEAD_DIM = 64
CHUNK = 64
LOG2_CHUNK = 6
N_HEADS = 8
N_PAIRS = N_HEADS // 2
W_MIX = N_HEADS * HEAD_DIM
IDX_HEADS = 8
IDX_DIM = 64
TOPK_MAX = 256
DECAY_LORA = 64
AAA_LORA = 64
GATE_LORA = 128
B_SHIFT_WIDTH = 3 * W_MIX + DECAY_LORA + AAA_LORA + GATE_LORA
DEPTH = 4
D_FF = 2816
ALPHA = (2 * DEPTH) ** 0.25
LN_EPS = 1e-5
GN_EPS = 1e-5
WKV_GN_EPS = 64e-5
LANES = 128
NEG_BIG = -1e30
SB_EXIT_LOG = -104.0
LOG2E = 1.4426950408889634
VMEM_LIMIT = 56 * 1024 * 1024
INT_MIN = -2 ** 31


def _cparams(*sem):
    return pltpu.CompilerParams(dimension_semantics=sem, vmem_limit_bytes=VMEM_LIMIT)


def _dot(a, b):
    return jnp.dot(a.astype(BF16), b.astype(BF16), preferred_element_type=F32)


def _dot_nt(a, b):
    return lax.dot_general(a.astype(BF16), b.astype(BF16), (((1,), (1,)), ((), ())),
                           preferred_element_type=F32)


def _split_dot(x, m):
    hi = x.astype(BF16)
    lo = (x - hi.astype(F32)).astype(BF16)
    return (jnp.dot(hi, m, preferred_element_type=F32) + jnp.dot(lo, m, preferred_element_type=F32))


def _split_dot_left(m, x):
    hi = x.astype(BF16)
    lo = (x - hi.astype(F32)).astype(BF16)
    return (jnp.dot(m, hi, preferred_element_type=F32) + jnp.dot(m, lo, preferred_element_type=F32))


def _softplus(x):
    return jnp.maximum(x, 0.0) + jnp.log(1.0 + jnp.exp(-jnp.abs(x)))


_sigmoid = jax.nn.sigmoid
_tanh = jnp.tanh


def _layernorm_rows(h, g, b):
    mu = jnp.mean(h, axis=-1, keepdims=True)
    d = h - mu
    var = jnp.mean(d * d, axis=-1, keepdims=True)
    return d * lax.rsqrt(var + LN_EPS) * g + b


def _proj_kernel(x_ref, w_ref, *o_refs):
    acc = jnp.dot(x_ref[...].astype(BF16), w_ref[...], preferred_element_type=F32)
    for o_ref in o_refs:
        o_ref[...] = acc.astype(o_ref.dtype)


def _proj(x, w, out_dtypes, tm, tn):
    m, k = x.shape
    n = w.shape[1]
    assert m % tm == 0 and n % tn == 0
    return pl.pallas_call(
        _proj_kernel,
        grid=(m // tm, n // tn),
        in_specs=[pl.BlockSpec((tm, k), lambda i, j: (i, 0)),
                  pl.BlockSpec((k, tn), lambda i, j: (0, j))],
        out_specs=[pl.BlockSpec((tm, tn), lambda i, j: (i, j)) for _ in out_dtypes],
        out_shape=[jax.ShapeDtypeStruct((m, n), dt) for dt in out_dtypes],
        compiler_params=_cparams("parallel", "arbitrary"),
    )(x, w)


def _split_bf16(a):
    hi = a.astype(BF16)
    return hi, (a - hi.astype(F32)).astype(BF16)


def _proj3_kernel(x_ref, wh_ref, wl_ref, o1_ref, o2_ref):
    xh, xl = _split_bf16(x_ref[...])
    acc = jnp.dot(xh, wh_ref[...], preferred_element_type=F32)
    acc = acc + jnp.dot(xh, wl_ref[...], preferred_element_type=F32)
    acc = acc + jnp.dot(xl, wh_ref[...], preferred_element_type=F32)
    n1 = o1_ref.shape[1]
    o1_ref[...] = acc[:, :n1]
    o2_ref[...] = acc[:, n1:]


def _proj3(x, w, n1, tm):
    m, k = x.shape
    n = w.shape[1]
    wh, wl = _split_bf16(w)
    return pl.pallas_call(
        _proj3_kernel,
        grid=(m // tm,),
        in_specs=[pl.BlockSpec((tm, k), lambda i: (i, 0)),
                  pl.BlockSpec((k, n), lambda i: (0, 0)), pl.BlockSpec((k, n), lambda i: (0, 0))],
        out_specs=[pl.BlockSpec((tm, n1), lambda i: (i, 0)), pl.BlockSpec((tm, n - n1), lambda i: (i, 0))],
        out_shape=[jax.ShapeDtypeStruct((m, n1), F32), jax.ShapeDtypeStruct((m, n - n1), F32)],
        compiler_params=_cparams("parallel"),
    )(x, wh, wl)


def _outproj_ln_kernel(o1_ref, o2_ref, w1_ref, w2_ref, x_ref, g_ref, b_ref, y_ref):
    acc = jnp.dot(o1_ref[...], w1_ref[...], preferred_element_type=F32)
    acc = acc + jnp.dot(o2_ref[...], w2_ref[...], preferred_element_type=F32)
    y_ref[...] = _layernorm_rows(ALPHA * x_ref[...] + acc, g_ref[...], b_ref[...])


def _outproj_ln(o1, o2, w, x, g, b, tm):
    m, d = x.shape
    k1, k2 = o1.shape[1], o2.shape[1]
    row = lambda i: (i, 0)
    fix = lambda i: (0, 0)
    return pl.pallas_call(
        _outproj_ln_kernel,
        grid=(m // tm,),
        in_specs=[pl.BlockSpec((tm, k1), row), pl.BlockSpec((tm, k2), row),
                  pl.BlockSpec((k1, d), fix), pl.BlockSpec((k2, d), fix),
                  pl.BlockSpec((tm, d), row), pl.BlockSpec((1, d), fix), pl.BlockSpec((1, d), fix)],
        out_specs=pl.BlockSpec((tm, d), row),
        out_shape=jax.ShapeDtypeStruct((m, d), F32),
        compiler_params=_cparams("parallel"),
    )(o1, o2, w[:k1], w[k1:], x, g.reshape(1, d), b.reshape(1, d))


def _ffn_kernel(x_ref, wg_ref, wu_ref, wd_ref, g_ref, b_ref, y_ref, xb_s, acc_s):
    j = pl.program_id(1)

    @pl.when(j == 0)
    def _():
        xb_s[...] = x_ref[...].astype(BF16)
        acc_s[...] = jnp.zeros_like(acc_s)

    xb = xb_s[...]
    hg = jnp.dot(xb, wg_ref[...], preferred_element_type=F32)
    hu = jnp.dot(xb, wu_ref[...], preferred_element_type=F32)
    h = (hg * _sigmoid(hg) * hu).astype(BF16)
    acc_s[...] += jnp.dot(h, wd_ref[...], preferred_element_type=F32)

    @pl.when(j == pl.num_programs(1) - 1)
    def _():
        y_ref[...] = _layernorm_rows(ALPHA * x_ref[...] + acc_s[...], g_ref[...], b_ref[...])


def _ffn_ln(x, wg, wu, wd, g, b, tm, tf):
    m, d = x.shape
    f = wg.shape[1]
    assert m % tm == 0 and f % tf == 0
    return pl.pallas_call(
        _ffn_kernel,
        grid=(m // tm, f // tf),
        in_specs=[pl.BlockSpec((tm, d), lambda i, j: (i, 0)),
                  pl.BlockSpec((d, tf), lambda i, j: (0, j)),
                  pl.BlockSpec((d, tf), lambda i, j: (0, j)),
                  pl.BlockSpec((tf, d), lambda i, j: (j, 0)),
                  pl.BlockSpec((1, d), lambda i, j: (0, 0)),
                  pl.BlockSpec((1, d), lambda i, j: (0, 0))],
        out_specs=pl.BlockSpec((tm, d), lambda i, j: (i, 0)),
        out_shape=jax.ShapeDtypeStruct((m, d), F32),
        scratch_shapes=[pltpu.VMEM((tm, d), BF16), pltpu.VMEM((tm, d), F32)],
        compiler_params=_cparams("parallel", "arbitrary"),
    )(x, wg, wu, wd, g.reshape(1, d), b.reshape(1, d))


def _even_kernel(pa_ref, pb_ref, shift0_ref, sret0_ref, swkv0_ref,
                 mu_ref, vec_ref, w2p_ref, a2p_ref, g2_ref, hsum_ref, lincl_ref, usuf_ref,
                 dmask_ref, qdec_ref, kdec_ref, cdec_ref,
                 oret_ref, ob_ref, sret_ref, swkv_ref, shift_ref,
                 sret_s, swkv_s, prev_s,
                 at_s, bt_s, kt_s, rt_s, bg_s, kg_s, v_s, gc_s, q_s, k_s, vr_s,
                 yret_s, ywkv_s, bonus_s, gate_s,
                 t_s, arb_s, wl_s, yl_s, sl_s, ol_s, rl_s):
    c = pl.program_id(1)
    tb = pb_ref.shape[1]
    n_chunks = tb // CHUNK

    @pl.when(c == 0)
    def _():
        sret_s[...] = sret0_ref[0]
        swkv_s[...] = swkv0_ref[0]
        prev_s[...] = shift0_ref[0]

    hsum = hsum_ref[...]

    def vec(i):
        return vec_ref[i:i + 1, :]

    w0, a0, k_k, k_a, r_k, g_ret, g_wkv = (vec(i) for i in range(7))

    pb = pb_ref[0]
    row = lax.broadcasted_iota(I32, (tb, 1), 0)
    prev = jnp.where(row == 0, prev_s[...], pltpu.roll(pb, 1, axis=0))
    prev_s[...] = pb[tb - 1:tb, :]
    mixed = pb + (prev - pb) * mu_ref[...]
    rb = mixed[:, 0:W_MIX]
    kb = mixed[:, W_MIX:2 * W_MIX]
    vb = mixed[:, 2 * W_MIX:3 * W_MIX]
    wa = mixed[:, 3 * W_MIX:3 * W_MIX + LANES]
    gl = mixed[:, 3 * W_MIX + LANES:3 * W_MIX + 2 * LANES]
    lane = lax.broadcasted_iota(I32, (1, LANES), 1)
    wa_t = jnp.where(lane < DECAY_LORA, _tanh(wa), wa).astype(BF16)
    w_log = -_softplus(-(w0 + jnp.dot(wa_t, w2p_ref[...], preferred_element_type=F32))) - 0.5
    logw = -jnp.exp(w_log)
    lr = _sigmoid(a0 + jnp.dot(wa_t, a2p_ref[...], preferred_element_type=F32))
    gate_s[...] = jnp.dot(_sigmoid(gl).astype(BF16), g2_ref[...], preferred_element_type=F32)
    kkv = kb * k_k
    kkn = kkv * lax.rsqrt(jnp.maximum(_split_dot(kkv * kkv, hsum), 1e-12))
    k4 = kb * (1.0 + (lr - 1.0) * k_a)
    bonus_s[...] = _split_dot(rb * k4 * r_k, hsum) * vb
    cum = _split_dot_left(lincl_ref[...], logw)
    suf = _split_dot_left(usuf_ref[...], logw)
    ginv = jnp.exp(-cum)
    gsuf = jnp.exp(suf)
    at_s[...] = (-kkn * jnp.exp(cum - logw)).astype(BF16)
    bt_s[...] = (kkn * lr * ginv).astype(BF16)
    kt_s[...] = (k4 * ginv).astype(BF16)
    rt_s[...] = (rb * jnp.exp(cum)).astype(BF16)
    bg_s[...] = (kkn * lr * gsuf).astype(BF16)
    kg_s[...] = (k4 * gsuf).astype(BF16)
    v_s[...] = vb.astype(BF16)
    gc_s[...] = jnp.exp(cum + suf)

    q_s[...] = pa_ref[0, :, 0:W_MIX].astype(BF16)
    k_s[...] = (pa_ref[0, :, W_MIX:2 * W_MIX] * HEAD_DIM ** -0.5).astype(BF16)
    vr_s[...] = pa_ref[0, :, 2 * W_MIX:3 * W_MIX].astype(BF16)

    head0 = lane < HEAD_DIM
    r2 = lax.broadcasted_iota(I32, (2 * CHUNK, 2 * CHUNK), 0)
    c2 = lax.broadcasted_iota(I32, (2 * CHUNK, 2 * CHUNK), 1)
    same = (r2 // CHUNK) == (c2 // CHUNK)
    strict = same & (r2 > c2)
    incl = same & (r2 >= c2)

    eye = (r2 == c2).astype(F32)
    pairs = range(N_PAIRS)
    lanes_of = [slice(LANES * p, LANES * (p + 1)) for p in pairs]

    def stacked(ref, rows):
        out = []
        for p in pairs:
            x = ref[rows, lanes_of[p]]
            z = jnp.zeros_like(x)
            out.append(jnp.concatenate([jnp.where(head0, x, z), jnp.where(head0, z, x)], axis=0))
        return out

    def local_terms(ci, carry):
        rows = pl.ds(pl.multiple_of(ci * CHUNK, CHUNK), CHUNK)
        a2, b2, k2, rr2 = stacked(at_s, rows), stacked(bt_s, rows), stacked(kt_s, rows), stacked(rt_s, rows)
        v2, kg2 = stacked(v_s, rows), stacked(kg_s, rows)
        g = [_dot_nt(jnp.concatenate([a2[p], rr2[p]], axis=0), jnp.concatenate([b2[p], k2[p]], axis=0))
             for p in pairs]
        npow = [jnp.where(strict, g[p][0:128, 0:128], 0.0).astype(BF16) for p in pairs]
        a_ak = [jnp.where(strict, g[p][0:128, 128:256], 0.0) for p in pairs]
        a_rk = [jnp.where(incl, g[p][128:256, 128:256], 0.0) for p in pairs]
        for p in pairs:
            arb_s[ci * N_PAIRS + p] = jnp.where(incl, g[p][128:256, 0:128], 0.0).astype(BF16)
            wl_s[ci * N_PAIRS + p] = _dot(a_ak[p], v2[p])
            yl_s[ci * N_PAIRS + p] = _dot(a_rk[p], v2[p])
            sl_s[ci * N_PAIRS + p] = _dot(v2[p].astype(F32).T, kg2[p])
        tinv = [eye + npow[p].astype(F32) for p in pairs]
        for lvl in range(1, 6):
            npow = [_dot(npow[p], npow[p]).astype(BF16) for p in pairs]
            tinv = [tinv[p] + _dot(tinv[p], npow[p]) for p in pairs]
        for p in pairs:
            t_s[ci * N_PAIRS + p] = tinv[p].astype(BF16)
        q2, kr2, vr2 = stacked(q_s, rows), stacked(k_s, rows), stacked(vr_s, rows)
        att = [_dot_nt(q2[p], kr2[p]) * dmask_ref[p] for p in pairs]
        for p in pairs:
            ol_s[ci * N_PAIRS + p] = _dot(att[p], vr2[p])
            rl_s[ci * N_PAIRS + p] = _dot((kr2[p].astype(F32) * kdec_ref[p]).T, vr2[p])
        return carry

    lax.fori_loop(0, n_chunks, local_terms, 0)

    def recur(ci, carry):
        r0 = pl.multiple_of(ci * CHUNK, CHUNK)
        rows = pl.ds(r0, CHUNK)
        a2, rr2, bg2, q2 = stacked(at_s, rows), stacked(rt_s, rows), stacked(bg_s, rows), stacked(q_s, rows)
        s = [swkv_s[p] for p in pairs]
        sr = [sret_s[p] for p in pairs]
        ps = [_dot_nt(jnp.concatenate([a2[p], rr2[p]], axis=0), s[p]) for p in pairs]
        u = [_dot(t_s[ci * N_PAIRS + p], ps[p][0:128] + wl_s[ci * N_PAIRS + p]).astype(BF16) for p in pairs]
        for p in pairs:
            swkv_s[p] = (s[p] * gc_s[pl.ds(r0, 1), lanes_of[p]] + _dot(u[p].astype(F32).T, bg2[p])
                         + sl_s[ci * N_PAIRS + p])
        for p in pairs:
            y2 = ps[p][128:256] + _dot(arb_s[ci * N_PAIRS + p], u[p]) + yl_s[ci * N_PAIRS + p]
            ywkv_s[rows, lanes_of[p]] = y2[0:CHUNK] + y2[CHUNK:2 * CHUNK]
        for p in pairs:
            o2 = ol_s[ci * N_PAIRS + p] + _dot(q2[p], sr[p]) * qdec_ref[p]
            yret_s[rows, lanes_of[p]] = o2[0:CHUNK] + o2[CHUNK:2 * CHUNK]
            sret_s[p] = sr[p] * cdec_ref[p] + rl_s[ci * N_PAIRS + p]
        return carry

    lax.fori_loop(0, n_chunks, recur, 0)

    inv = 1.0 / HEAD_DIM
    yr = yret_s[...]
    d = yr - _split_dot(yr, hsum) * inv
    var = _split_dot(d * d, hsum) * inv
    ga = pa_ref[0, :, 3 * W_MIX:4 * W_MIX]
    oret_ref[0] = (d * lax.rsqrt(var + GN_EPS) * g_ret * (ga * _sigmoid(ga))).astype(oret_ref.dtype)
    yw = ywkv_s[...]
    d = yw - _split_dot(yw, hsum) * inv
    var = _split_dot(d * d, hsum) * inv
    ob_ref[0] = ((d * lax.rsqrt(var + WKV_GN_EPS) * g_wkv + bonus_s[...]) * gate_s[...]).astype(ob_ref.dtype)

    @pl.when(c == pl.num_programs(1) - 1)
    def _():
        sret_ref[0] = sret_s[...]
        swkv_ref[0] = swkv_s[...]
        shift_ref[0] = prev_s[...]


def _pair_blockdiag(s):
    b = s.shape[0]
    s = s.reshape(b, N_PAIRS, 2, HEAD_DIM, HEAD_DIM)
    z = jnp.zeros_like(s[:, :, 0])
    top = jnp.concatenate([s[:, :, 0], z], axis=-1)
    bot = jnp.concatenate([z, s[:, :, 1]], axis=-1)
    return jnp.concatenate([top, bot], axis=-2)


def _pair_unblock(s):
    b = s.shape[0]
    h0 = s[:, :, 0:HEAD_DIM, 0:HEAD_DIM]
    h1 = s[:, :, HEAD_DIM:, HEAD_DIM:]
    return jnp.stack([h0, h1], axis=2).reshape(b, N_HEADS, HEAD_DIM, HEAD_DIM)


def _retention_constants():
    log_g = jnp.log1p(-jnp.exp2(-5.0 - jnp.arange(N_HEADS, dtype=F32)))
    lg = jnp.repeat(log_g.reshape(N_PAIRS, 2), CHUNK, axis=1)
    t = jnp.tile(jnp.arange(CHUNK, dtype=F32), 2)
    hd = jnp.repeat(jnp.arange(2), CHUNK)
    diff = t[:, None] - t[None, :]
    same = hd[:, None] == hd[None, :]
    dmask = jnp.where(same[None] & (diff[None] >= 0),
                      jnp.exp(lg[:, :, None] * jnp.maximum(diff, 0.0)[None]), 0.0)
    ones = jnp.ones((1, 1, LANES), F32)
    qdec = jnp.exp(lg * (t[None] + 1.0))[:, :, None] * ones
    kdec = jnp.exp(lg * (CHUNK - 1.0 - t[None]))[:, :, None] * ones
    cdec = jnp.exp(lg * CHUNK)[:, :, None] * ones
    return dmask, qdec, kdec, cdec


def _chunk_sum_matrices(tb):
    r = jnp.arange(tb)
    same = (r[:, None] // CHUNK) == (r[None, :] // CHUNK)
    lincl = (same & (r[None, :] <= r[:, None])).astype(BF16)
    usuf = (same & (r[None, :] > r[:, None])).astype(BF16)
    return lincl, usuf


def _even_mixer(pa, pb, shift0, sret0, swkv0, prm, tb):
    b, t, _ = pa.shape
    assert t % tb == 0 and tb % CHUNK == 0
    lincl, usuf = _chunk_sum_matrices(tb)
    dmask, qdec, kdec, cdec = _retention_constants()
    hd = jnp.arange(W_MIX) // HEAD_DIM
    hsum = (hd[:, None] == hd[None, :]).astype(BF16)

    def fixed(shape):
        zero = (0,) * len(shape)
        return pl.BlockSpec(shape, lambda i, c: zero)

    def per_batch(shape):
        zero = (0,) * len(shape)
        return pl.BlockSpec((1,) + shape, lambda i, c: (i,) + zero)

    def per_block(w):
        return pl.BlockSpec((1, tb, w), lambda i, c: (i, c, 0))

    pair_state = (N_PAIRS, LANES, LANES)
    wide = lambda dt: pltpu.VMEM((tb, W_MIX), dt)
    per_chunk = lambda dt: pltpu.VMEM((tb // CHUNK * N_PAIRS, LANES, LANES), dt)
    outs = pl.pallas_call(
        _even_kernel,
        grid=(b, t // tb),
        in_specs=[per_block(4 * W_MIX), per_block(B_SHIFT_WIDTH),
                  per_batch((1, B_SHIFT_WIDTH)), per_batch(pair_state), per_batch(pair_state),
                  fixed((1, B_SHIFT_WIDTH)), fixed((8, W_MIX)),
                  fixed((LANES, W_MIX)), fixed((LANES, W_MIX)), fixed((GATE_LORA, W_MIX)),
                  fixed((W_MIX, W_MIX)), fixed((tb, tb)), fixed((tb, tb)),
                  fixed(pair_state), fixed(pair_state), fixed(pair_state), fixed(pair_state)],
        out_specs=[per_block(W_MIX), per_block(W_MIX),
                   per_batch(pair_state), per_batch(pair_state), per_batch((1, B_SHIFT_WIDTH))],
        out_shape=[jax.ShapeDtypeStruct((b, t, W_MIX), BF16), jax.ShapeDtypeStruct((b, t, W_MIX), BF16),
                   jax.ShapeDtypeStruct((b,) + pair_state, F32), jax.ShapeDtypeStruct((b,) + pair_state, F32),
                   jax.ShapeDtypeStruct((b, 1, B_SHIFT_WIDTH), F32)],
        scratch_shapes=[pltpu.VMEM(pair_state, F32), pltpu.VMEM(pair_state, F32),
                        pltpu.VMEM((1, B_SHIFT_WIDTH), F32)]
                       + [wide(BF16)] * 7 + [wide(F32)] + [wide(BF16)] * 3 + [wide(F32)] * 4
                       + [per_chunk(BF16)] * 2 + [per_chunk(F32)] * 5,
        compiler_params=_cparams("parallel", "arbitrary"),
    )(pa, pb, shift0, _pair_blockdiag(sret0), _pair_blockdiag(swkv0),
      prm["mu"], prm["vec"], prm["w2p"], prm["a2p"], prm["g2"], hsum, lincl, usuf,
      dmask, qdec, kdec, cdec)
    o_ret, o_b, sret, swkv, shift = outs
    return o_ret, o_b, shift, _pair_unblock(sret), _pair_unblock(swkv)


SUBLANES = 8


def _dsa_kernel(qc_ref, qi_ref, wit_ref, kc_ref, vat_ref, ki_ref, slope_ref, o_ref,
                key_s, qt_s, q3_s, acc_s, *, tq, tk, tka, past, topk):
    i = pl.program_id(1)
    q0 = past + i * tq
    n_kb = (q0 + tq + tk - 1) // tk

    qpos = q0 + lax.broadcasted_iota(I32, (1, tq), 1)
    qchunk = qpos >> LOG2_CHUNK
    sub_k = lax.broadcasted_iota(I32, (tk, 1), 0)
    sub_k_chunk = sub_k >> LOG2_CHUNK

    qt_s[...] = qc_ref[0].astype(F32).T.astype(BF16)
    qit_hi, qit_lo = _split_bf16(qi_ref[0].T)
    zero_rows = jnp.zeros((IDX_DIM, tq), BF16)
    for h in range(IDX_HEADS):
        rows = slice(IDX_DIM * h, IDX_DIM * (h + 1))
        q3_s[4 * IDX_DIM * h:4 * IDX_DIM * (h + 1), :] = jnp.concatenate(
            [qit_hi[rows], qit_lo[rows], qit_hi[rows], zero_rows], axis=0)

    wit = wit_ref[0] * (IDX_DIM ** -0.5 * IDX_HEADS ** -0.5)

    def score_body(kb, carry):
        k0 = pl.multiple_of(kb * tk, tk)
        ki = ki_ref[0, pl.ds(k0, tk), :]
        s = jnp.zeros((tk, tq), F32)
        for h in range(IDX_HEADS):
            qit = q3_s[4 * IDX_DIM * h:4 * IDX_DIM * (h + 1), :]
            s = s + jnp.maximum(jnp.dot(ki, qit, preferred_element_type=F32), 0.0) * wit[h:h + 1, :]
        s = s + 0.0
        adm = sub_k_chunk <= (qchunk - k0 // CHUNK)
        s = jnp.where(adm, s, -jnp.inf)
        bits = pltpu.bitcast(s, I32)
        key_s[pl.ds(k0, tk), :] = bits ^ ((bits >> 31) & 0x7FFFFFFF)
        return carry

    lax.fori_loop(0, n_kb, score_body, 0)

    def count(pred):
        def body(kb, acc):
            k0 = pl.multiple_of(kb * tk, tk)
            hit = jnp.where(pred(key_s[pl.ds(k0, tk), :], k0 + sub_k), 1.0, 0.0)
            parts = [hit[SUBLANES * g:SUBLANES * (g + 1), :] for g in range(tk // SUBLANES)]
            while len(parts) > 1:
                parts = [parts[n] + parts[n + 1] for n in range(0, len(parts), 2)]
            return acc + parts[0]
        acc = lax.fori_loop(0, n_kb, body, jnp.zeros((SUBLANES, tq), F32))
        return jnp.sum(acc, axis=0, keepdims=True)

    base = jnp.where(count(lambda key, kp: key >= 0) >= topk, 0, INT_MIN).astype(I32)

    def bit_body(b, base):
        cand = base | (jnp.int32(1) << (30 - b))
        return jnp.where(count(lambda key, kp: key >= cand) >= topk, cand, base)

    thr = lax.fori_loop(0, 31, bit_body, base)
    need = topk - count(lambda key, kp: key > thr)
    n_eq = count(lambda key, kp: key == thr)

    def tie_cut(_):
        def tbit(b, m):
            cand = m | (jnp.int32(1) << (14 - b))
            return jnp.where(count(lambda key, kp: (key == thr) & (kp < cand)) < need, cand, m)
        return lax.fori_loop(0, 15, tbit, jnp.zeros((1, tq), I32))

    last_eq = lax.cond(jnp.max(n_eq - need) > 0.0, tie_cut,
                       lambda _: jnp.full((1, tq), 2 ** 30, I32), 0)

    acc_s[...] = jnp.zeros(acc_s.shape, F32)
    sub_a = lax.broadcasted_iota(I32, (tka, 1), 0)
    sub_a_chunk = sub_a >> LOG2_CHUNK
    sub_af = sub_a.astype(F32)
    qposf = qpos.astype(F32)
    n_kba = (q0 + tq + tka - 1) // tka
    heads = range(N_HEADS)

    def att_body(kb, m):
        k0 = pl.multiple_of(kb * tka, tka)
        kpos = k0 + sub_a
        key = key_s[pl.ds(k0, tka), :]
        sel = ((sub_a_chunk <= (qchunk - k0 // CHUNK))
               & ((key > thr) | ((key == thr) & (kpos <= last_eq))))
        ndist = -jnp.abs(qposf - (sub_af + k0.astype(F32)))
        kc = kc_ref[0, pl.ds(k0, tka), :]
        vat = vat_ref[0, :, pl.ds(k0, tka)]
        dots = [jnp.dot(kc, qt_s[HEAD_DIM * h:HEAD_DIM * (h + 1), :], preferred_element_type=F32)
                for h in heads]
        logit = [jnp.where(sel, dots[h] + slope_ref[h] * ndist, NEG_BIG) for h in heads]
        m_old = [m[h:h + 1, :] for h in heads]
        m_new = [jnp.maximum(m_old[h], jnp.max(logit[h], axis=0, keepdims=True)) for h in heads]
        p = [jnp.exp2(logit[h] - m_new[h]).astype(BF16) for h in heads]
        pv = [jnp.dot(vat, p[h], preferred_element_type=F32) for h in heads]
        for h in heads:
            acc_s[h] = jnp.exp2(m_old[h] - m_new[h]) * acc_s[h] + pv[h]
        return jnp.concatenate(m_new, axis=0)

    lax.fori_loop(0, n_kba, att_body, jnp.full((N_HEADS, tq), NEG_BIG, F32))
    outs = []
    for h in heads:
        acc = acc_s[h]
        outs.append(acc[0:HEAD_DIM, :] / acc[HEAD_DIM:HEAD_DIM + 1, :])
    o_ref[0] = jnp.concatenate(outs, axis=0).T.astype(o_ref.dtype)


def _dsa(pq, qi, wi, kc, vc, ki, past, tq, tk, tka):
    b, t, _ = pq.shape
    lp = kc.shape[1]
    ki_hi, ki_lo = _split_bf16(ki)
    ki3 = jnp.concatenate([ki_hi, ki_hi, ki_lo, jnp.zeros_like(ki_hi)], axis=-1)
    n_keys = past + t
    topk = min(TOPK_MAX, n_keys // 4)
    assert t % tq == 0 and tq % CHUNK == 0 and lp % tk == 0 and tk % tka == 0 and lp >= n_keys
    slopes = jnp.exp2(-8.0 * (jnp.arange(N_HEADS, dtype=F32) + 1.0) / N_HEADS) * LOG2E
    vat = jnp.concatenate([jnp.swapaxes(vc, 1, 2), jnp.ones((b, 1, lp), BF16),
                           jnp.zeros((b, LANES - HEAD_DIM - 1, lp), BF16)], axis=1)
    wit = jnp.swapaxes(wi, 1, 2)
    qblk = lambda w, col: pl.BlockSpec((1, tq, w), lambda bi, i: (bi, i, col))
    kblk = lambda w: pl.BlockSpec((1, lp, w), lambda bi, i: (bi, 0, 0))
    return pl.pallas_call(
        functools.partial(_dsa_kernel, tq=tq, tk=tk, tka=tka, past=past, topk=topk),
        grid=(b, t // tq),
        in_specs=[qblk(W_MIX, 0), qblk(W_MIX, 0),
                  pl.BlockSpec((1, IDX_HEADS, tq), lambda bi, i: (bi, 0, i)),
                  kblk(HEAD_DIM), pl.BlockSpec((1, LANES, lp), lambda bi, i: (bi, 0, 0)),
                  kblk(4 * IDX_DIM), pl.BlockSpec(memory_space=pltpu.SMEM)],
        out_specs=qblk(W_MIX, 0),
        out_shape=jax.ShapeDtypeStruct((b, t, W_MIX), BF16),
        scratch_shapes=[pltpu.VMEM((lp, tq), I32), pltpu.VMEM((W_MIX, tq), BF16),
                        pltpu.VMEM((4 * IDX_DIM * IDX_HEADS, tq), BF16),
                        pltpu.VMEM((N_HEADS, LANES, tq), F32)],
        compiler_params=_cparams("parallel", "arbitrary"),
    )(pq, qi, wit, kc, vat, ki3, slopes)


def _sb_kernel(q_ref, k_ref, v_ref, o_ref, *, tq, tk, past):
    i = pl.program_id(2)
    q0 = past + i * tq
    qpos = q0 + lax.broadcasted_iota(I32, (tq, 1), 0)
    lane_k = lax.broadcasted_iota(I32, (1, tk), 1)
    lane = lax.broadcasted_iota(I32, (1, LANES), 1)
    head0 = lane < HEAD_DIM
    rk = lax.broadcasted_iota(I32, (tk, tk), 0)
    ck = lax.broadcasted_iota(I32, (tk, tk), 1)
    later = (rk > ck).astype(BF16)

    q = q_ref[0]
    zq = jnp.zeros_like(q)
    q_heads = (jnp.where(head0, q, zq), jnp.where(head0, zq, q))

    j_hi = (q0 + tq - 2) // tk
    n_full = q0 // tk

    def blocks(js, carry):
        heads = range(2)
        k0 = [pl.multiple_of(j * tk, tk) for j, _ in js]
        kblk = [k_ref[0, pl.ds(k, tk), :] for k in k0]
        vblk = [v_ref[0, pl.ds(k, tk), :] for k in k0]
        z = [[_dot_nt(q_heads[h], kb) for h in heads] for kb in kblk]
        lrem, lsig, causal = [], [], []
        for n, (_, masked) in enumerate(js):
            causal.append((k0[n] + lane_k) < qpos if masked else None)
            row_r, row_s = [], []
            for h in heads:
                r = -(jnp.maximum(z[n][h], 0.0) + jnp.log(1.0 + jnp.exp(-jnp.abs(z[n][h]))))
                row_s.append(z[n][h] + r)
                row_r.append(jnp.where(causal[n], r, 0.0) if masked else r)
            lrem.append(row_r)
            lsig.append(row_s)
        run = [carry[h][0] for h in heads]
        acc = [carry[h][1] for h in heads]
        run_in = []
        for n in range(len(js)):
            run_in.append(list(run))
            run = [run[h] + jnp.sum(lrem[n][h], axis=1, keepdims=True) for h in heads]
        after = [[jnp.dot(lrem[n][h].astype(BF16), later, preferred_element_type=F32) for h in heads]
                 for n in range(len(js))]
        for n, (_, masked) in enumerate(js):
            for h in heads:
                att = jnp.exp(lsig[n][h] + after[n][h] + run_in[n][h])
                if masked:
                    att = jnp.where(causal[n], att, 0.0)
                acc[h] = acc[h] + _dot(att, vblk[n])
        return tuple((run[h], acc[h]) for h in heads)

    init = tuple((jnp.zeros((tq, 1), F32), jnp.zeros((tq, LANES), F32)) for _ in range(2))
    carry = lax.cond(n_full > 0,
                     lambda: blocks([(j_hi, True), (j_hi - 1, False)], init),
                     lambda: blocks([(j_hi, True)], init))

    def more(state):
        s, c = state
        return (s < n_full) & (jnp.max(jnp.maximum(c[0][0], c[1][0])) > SB_EXIT_LOG)

    _, carry = lax.while_loop(more, lambda st: (st[0] + 1, blocks([(n_full - 1 - st[0], False)], st[1])),
                              (jnp.int32(1), carry))
    o_ref[0] = jnp.where(head0, carry[0][1], carry[1][1]).astype(o_ref.dtype)


def _stickbreak(q, q_col, k, v, kv_cols, past, tq, tk):
    b, t, _ = q.shape
    lp = k.shape[1]
    assert t % tq == 0 and lp % tk == 0 and lp >= past + t
    assert tk % tq == 0 and past % tq == 0
    k_col, v_col = kv_cols
    return pl.pallas_call(
        functools.partial(_sb_kernel, tq=tq, tk=tk, past=past),
        grid=(b, N_PAIRS, t // tq),
        in_specs=[pl.BlockSpec((1, tq, LANES), lambda bi, p, i: (bi, i, q_col + p)),
                  pl.BlockSpec((1, lp, LANES), lambda bi, p, i: (bi, 0, k_col + p)),
                  pl.BlockSpec((1, lp, LANES), lambda bi, p, i: (bi, 0, v_col + p))],
        out_specs=pl.BlockSpec((1, tq, LANES), lambda bi, p, i: (bi, i, p)),
        out_shape=jax.ShapeDtypeStruct((b, t, W_MIX), BF16),
        compiler_params=_cparams("parallel", "parallel", "arbitrary"),
    )(q, k, v)


def _row_tile(m, cap):
    t = cap
    while m % t:
        t //= 2
    return t


def _even_layer(x, shift0, sret0, swkv0, w_in, w_out, prm, tb):
    b, t, d = x.shape
    xf = x.reshape(b * t, d)
    tm = _row_tile(b * t, 1024)
    (pa,) = _proj(xf, w_in[:, :4 * W_MIX], (F32,), tm, 2 * W_MIX)
    (pb,) = _proj(xf, w_in[:, 4 * W_MIX:], (F32,), tm, B_SHIFT_WIDTH // 2)
    o_ret, o_b, shift, sret, swkv = _even_mixer(pa.reshape(b, t, -1), pb.reshape(b, t, -1),
                                                shift0, sret0, swkv0, prm, tb)
    return o_ret.reshape(b * t, -1), o_b.reshape(b * t, -1), w_out, (shift, sret, swkv)


def _pad_rows(a, lp):
    return jnp.pad(a, ((0, 0), (0, lp - a.shape[1]), (0, 0)))


def _odd_layer(x, caches, w_q, w_kv, w_small, w_sel, w_out, tq_dsa, tq_sb, tk_dsa, tka_dsa, tk_sb):
    b, t, d = x.shape
    xf = x.reshape(b * t, d)
    tm = _row_tile(b * t, 1024)
    (pq16,) = _proj(xf, w_q, (BF16,), tm, w_q.shape[1] // 2)
    pkv32, pkv16 = _proj(xf, w_kv, (F32, BF16), tm, w_kv.shape[1] // 2)
    (ps32,) = _proj(xf, w_small, (F32,), tm, w_small.shape[1])
    qi, sel_small = _proj3(xf, w_sel, IDX_HEADS * IDX_DIM, tm)
    pq16 = pq16.reshape(b, t, -1)
    pkv32 = pkv32.reshape(b, t, -1)
    pkv16 = pkv16.reshape(b, t, -1)
    ps32 = ps32.reshape(b, t, -1)
    sel_small = sel_small.reshape(b, t, -1)
    kc, vc = ps32[..., :HEAD_DIM], ps32[..., HEAD_DIM:]
    ki = sel_small[..., :IDX_DIM]
    wi = sel_small[..., IDX_DIM:IDX_DIM + IDX_HEADS]
    new_rows = (kc, vc, ki, pkv32[..., :W_MIX].reshape(b, t, N_HEADS, HEAD_DIM),
                pkv32[..., W_MIX:].reshape(b, t, N_HEADS, HEAD_DIM))
    kc_a, vc_a, ki_a = kc.astype(BF16), vc.astype(BF16), ki
    if caches is None:
        past = 0
        k_sb = v_sb = pkv16
        sb_cols = (0, N_PAIRS)
    else:
        past = caches[0].shape[1]
        kc_a = jnp.concatenate([caches[0].astype(BF16), kc_a], axis=1)
        vc_a = jnp.concatenate([caches[1].astype(BF16), vc_a], axis=1)
        ki_a = jnp.concatenate([caches[2], ki_a], axis=1)
        lp_sb = -(-(past + t) // tk_sb) * tk_sb
        tail = jnp.zeros((b, lp_sb - past - t, W_MIX), BF16)
        k_sb, v_sb = (jnp.concatenate([c.reshape(b, past, -1).astype(BF16), n, tail], axis=1)
                      for c, n in zip(caches[3:], (pkv16[..., :W_MIX], pkv16[..., W_MIX:])))
        sb_cols = (0, 0)
    n_keys = past + t
    lp_dsa = -(-n_keys // tk_dsa) * tk_dsa
    kc_a, vc_a, ki_a = (_pad_rows(a, lp_dsa) for a in (kc_a, vc_a, ki_a))
    oc = _dsa(pq16, qi.reshape(b, t, -1), wi, kc_a, vc_a, ki_a, past, tq_dsa, tk_dsa, tka_dsa)
    od = _stickbreak(pq16, N_PAIRS, k_sb, v_sb, sb_cols, past, tq_sb, tk_sb)
    return oc.reshape(b * t, -1), od.reshape(b * t, -1), w_out, new_rows


def kernel(x_prompt, x_sample, state_shift, state_ret, state_wkv, cache_dsa_k, cache_dsa_v, cache_idx_k,
           cache_sb_k, cache_sb_v, w_in_even, w_out_even, gn_ret_g, rwkv_mu, rwkv_w0, rwkv_w2, rwkv_a0,
           rwkv_a2, rwkv_g2, rwkv_kk, rwkv_ka, rwkv_rk, gn_wkv_g, w_in_odd, w_out_odd, ln1_g, ln1_b,
           ln2_g, ln2_b, ffn_wg, ffn_wu, ffn_wd):
    xp, xs = x_prompt, x_sample
    bp, tp, d = xp.shape
    bs, ts, _ = xs.shape
    p_even, s_even = [[], [], []], [[], [], []]
    p_odd, s_odd = [[] for _ in range(5)], [[] for _ in range(5)]

    for l in range(DEPTH):
        if l % 2 == 0:
            e = l // 2
            zpad = jnp.zeros((DECAY_LORA, W_MIX), F32)
            prm = dict(
                mu=rwkv_mu[e].reshape(1, -1),
                vec=jnp.stack([rwkv_w0[e], rwkv_a0[e], rwkv_kk[e], rwkv_ka[e], rwkv_rk[e], gn_ret_g[e],
                               gn_wkv_g[e], jnp.zeros((W_MIX,), F32)]),
                w2p=jnp.concatenate([rwkv_w2[e], zpad], axis=0).astype(BF16),
                a2p=jnp.concatenate([zpad, rwkv_a2[e]], axis=0).astype(BF16),
                g2=rwkv_g2[e].astype(BF16))
            w_in = w_in_even[e].astype(BF16)
            w_out = w_out_even[e].astype(BF16)
            zeros = lambda *s: jnp.zeros(s, F32)
            o1p, o2p, wo, st_p = _even_layer(xp, zeros(bp, 1, B_SHIFT_WIDTH),
                                             zeros(bp, N_HEADS, HEAD_DIM, HEAD_DIM),
                                             zeros(bp, N_HEADS, HEAD_DIM, HEAD_DIM), w_in, w_out, prm, 256)
            o1s, o2s, _, st_s = _even_layer(xs, state_shift[e], state_ret[e], state_wkv[e], w_in, w_out,
                                            prm, ts)
            for n in range(3):
                p_even[n].append(st_p[n])
                s_even[n].append(st_s[n])
        else:
            o = l // 2
            w = w_in_odd[o]
            seg = {}
            off = 0
            for name, sz in (("qc", W_MIX), ("kc", HEAD_DIM), ("vc", HEAD_DIM), ("qi", IDX_HEADS * IDX_DIM),
                             ("ki", IDX_DIM), ("wi", IDX_HEADS), ("qd", W_MIX), ("kd", W_MIX), ("vd", W_MIX)):
                seg[name] = w[:, off:off + sz]
                off += sz
            q_scale = HEAD_DIM ** -0.5
            w_q = jnp.concatenate([seg["qc"] * (q_scale * LOG2E), seg["qd"] * q_scale], axis=1).astype(BF16)
            w_kv = jnp.concatenate([seg["kd"], seg["vd"]], axis=1).astype(BF16)
            w_small = jnp.concatenate([seg["kc"], seg["vc"]], axis=1).astype(BF16)
            w_sel = jnp.concatenate([seg["qi"], seg["ki"], seg["wi"],
                                     jnp.zeros((d, LANES - IDX_DIM - IDX_HEADS), F32)], axis=1)
            w_out = w_out_odd[o].astype(BF16)
            o1p, o2p, wo, rows_p = _odd_layer(xp, None, w_q, w_kv, w_small, w_sel, w_out,
                                              256, 128, 512, 256, 256)
            o1s, o2s, _, rows_s = _odd_layer(xs, (cache_dsa_k[o], cache_dsa_v[o], cache_idx_k[o],
                                                  cache_sb_k[o], cache_sb_v[o]),
                                             w_q, w_kv, w_small, w_sel, w_out, ts, ts, 512, 256, 256)
            for n in range(5):
                p_odd[n].append(rows_p[n])
                s_odd[n].append(rows_s[n])

        xpf = _outproj_ln(o1p, o2p, wo, xp.reshape(bp * tp, d), ln1_g[l], ln1_b[l], 512)
        xsf = _outproj_ln(o1s, o2s, wo, xs.reshape(bs * ts, d), ln1_g[l], ln1_b[l], 512)
        wg, wu, wd = ffn_wg[l].astype(BF16), ffn_wu[l].astype(BF16), ffn_wd[l].astype(BF16)
        xp = _ffn_ln(xpf, wg, wu, wd, ln2_g[l], ln2_b[l], 1024, 256).reshape(bp, tp, d)
        xs = _ffn_ln(xsf, wg, wu, wd, ln2_g[l], ln2_b[l], 1024, 256).reshape(bs, ts, d)

    stack = lambda lists: [jnp.stack(a) for a in lists]
    return (xp, xs, *stack(p_even), *stack(p_odd), *stack(s_even), *stack(s_odd))
```

```python
import functools
import math

import jax
import jax.numpy as jnp
from jax import lax
from jax.experimental import pallas as pl
from jax.experimental.pallas import tpu as pltpu

F32 = jnp.float32
BF16 = jnp.bfloat16
I32 = jnp.int32

D_MODEL = 1024
HEAD_DIM = 64
CHUNK = 64
LOG2_CHUNK = 6
N_HEADS = 8
N_PAIRS = N_HEADS // 2
W_MIX = N_HEADS * HEAD_DIM
IDX_HEADS = 8
IDX_DIM = 64
TOPK_MAX = 256
DECAY_LORA = 64
AAA_LORA = 64
GATE_LORA = 128
B_SHIFT_WIDTH = 3 * W_MIX + DECAY_LORA + AAA_LORA + GATE_LORA
DEPTH = 4
D_FF = 2816
ALPHA = (2 * DEPTH) ** 0.25
LN_EPS = 1e-5
GN_EPS = 1e-5
WKV_GN_EPS = 64e-5
LANES = 128
NEG_BIG = -1e30
SB_EXIT_LOG = -104.0
LOG2E = 1.4426950408889634
VMEM_LIMIT = 56 * 1024 * 1024
INT_MIN = -2 ** 31


def _cparams(*sem):
    return pltpu.CompilerParams(dimension_semantics=sem, vmem_limit_bytes=VMEM_LIMIT)


def _dot(a, b):
    return jnp.dot(a.astype(BF16), b.astype(BF16), preferred_element_type=F32)


def _dot_nt(a, b):
    return lax.dot_general(a.astype(BF16), b.astype(BF16), (((1,), (1,)), ((), ())),
                           preferred_element_type=F32)


def _split_dot(x, m):
    hi = x.astype(BF16)
    lo = (x - hi.astype(F32)).astype(BF16)
    return (jnp.dot(hi, m, preferred_element_type=F32) + jnp.dot(lo, m, preferred_element_type=F32))


def _split_dot_left(m, x):
    hi = x.astype(BF16)
    lo = (x - hi.astype(F32)).astype(BF16)
    return (jnp.dot(m, hi, preferred_element_type=F32) + jnp.dot(m, lo, preferred_element_type=F32))


def _softplus(x):
    return jnp.maximum(x, 0.0) + jnp.log(1.0 + jnp.exp(-jnp.abs(x)))


_sigmoid = jax.nn.sigmoid
_tanh = jnp.tanh


def _layernorm_rows(h, g, b):
    mu = jnp.mean(h, axis=-1, keepdims=True)
    d = h - mu
    var = jnp.mean(d * d, axis=-1, keepdims=True)
    return d * lax.rsqrt(var + LN_EPS) * g + b


def _proj_kernel(x_ref, w_ref, *o_refs):
    acc = jnp.dot(x_ref[...].astype(BF16), w_ref[...], preferred_element_type=F32)
    for o_ref in o_refs:
        o_ref[...] = acc.astype(o_ref.dtype)


def _proj(x, w, out_dtypes, tm, tn):
    m, k = x.shape
    n = w.shape[1]
    assert m % tm == 0 and n % tn == 0
    return pl.pallas_call(
        _proj_kernel,
        grid=(m // tm, n // tn),
        in_specs=[pl.BlockSpec((tm, k), lambda i, j: (i, 0)),
                  pl.BlockSpec((k, tn), lambda i, j: (0, j))],
        out_specs=[pl.BlockSpec((tm, tn), lambda i, j: (i, j)) for _ in out_dtypes],
        out_shape=[jax.ShapeDtypeStruct((m, n), dt) for dt in out_dtypes],
        compiler_params=_cparams("parallel", "arbitrary"),
    )(x, w)


def _split_bf16(a):
    hi = a.astype(BF16)
    return hi, (a - hi.astype(F32)).astype(BF16)


def _proj3_kernel(x_ref, wh_ref, wl_ref, o1_ref, o2_ref):
    xh, xl = _split_bf16(x_ref[...])
    acc = jnp.dot(xh, wh_ref[...], preferred_element_type=F32)
    acc = acc + jnp.dot(xh, wl_ref[...], preferred_element_type=F32)
    acc = acc + jnp.dot(xl, wh_ref[...], preferred_element_type=F32)
    n1 = o1_ref.shape[1]
    o1_ref[...] = acc[:, :n1]
    o2_ref[...] = acc[:, n1:]


def _proj3(x, w, n1, tm):
    m, k = x.shape
    n = w.shape[1]
    wh, wl = _split_bf16(w)
    return pl.pallas_call(
        _proj3_kernel,
        grid=(m // tm,),
        in_specs=[pl.BlockSpec((tm, k), lambda i: (i, 0)),
                  pl.BlockSpec((k, n), lambda i: (0, 0)), pl.BlockSpec((k, n), lambda i: (0, 0))],
        out_specs=[pl.BlockSpec((tm, n1), lambda i: (i, 0)), pl.BlockSpec((tm, n - n1), lambda i: (i, 0))],
        out_shape=[jax.ShapeDtypeStruct((m, n1), F32), jax.ShapeDtypeStruct((m, n - n1), F32)],
        compiler_params=_cparams("parallel"),
    )(x, wh, wl)


def _outproj_ln_kernel(o1_ref, o2_ref, w1_ref, w2_ref, x_ref, g_ref, b_ref, y_ref):
    acc = jnp.dot(o1_ref[...], w1_ref[...], preferred_element_type=F32)
    acc = acc + jnp.dot(o2_ref[...], w2_ref[...], preferred_element_type=F32)
    y_ref[...] = _layernorm_rows(ALPHA * x_ref[...] + acc, g_ref[...], b_ref[...])


def _outproj_ln(o1, o2, w, x, g, b, tm):
    m, d = x.shape
    k1, k2 = o1.shape[1], o2.shape[1]
    row = lambda i: (i, 0)
    fix = lambda i: (0, 0)
    return pl.pallas_call(
        _outproj_ln_kernel,
        grid=(m // tm,),
        in_specs=[pl.BlockSpec((tm, k1), row), pl.BlockSpec((tm, k2), row),
                  pl.BlockSpec((k1, d), fix), pl.BlockSpec((k2, d), fix),
                  pl.BlockSpec((tm, d), row), pl.BlockSpec((1, d), fix), pl.BlockSpec((1, d), fix)],
        out_specs=pl.BlockSpec((tm, d), row),
        out_shape=jax.ShapeDtypeStruct((m, d), F32),
        compiler_params=_cparams("parallel"),
    )(o1, o2, w[:k1], w[k1:], x, g.reshape(1, d), b.reshape(1, d))


def _ffn_kernel(x_ref, wg_ref, wu_ref, wd_ref, g_ref, b_ref, y_ref, xb_s, acc_s):
    j = pl.program_id(1)

    @pl.when(j == 0)
    def _():
        xb_s[...] = x_ref[...].astype(BF16)
        acc_s[...] = jnp.zeros_like(acc_s)

    xb = xb_s[...]
    hg = jnp.dot(xb, wg_ref[...], preferred_element_type=F32)
    hu = jnp.dot(xb, wu_ref[...], preferred_element_type=F32)
    h = (hg * _sigmoid(hg) * hu).astype(BF16)
    acc_s[...] += jnp.dot(h, wd_ref[...], preferred_element_type=F32)

    @pl.when(j == pl.num_programs(1) - 1)
    def _():
        y_ref[...] = _layernorm_rows(ALPHA * x_ref[...] + acc_s[...], g_ref[...], b_ref[...])


def _ffn_ln(x, wg, wu, wd, g, b, tm, tf):
    m, d = x.shape
    f = wg.shape[1]
    assert m % tm == 0 and f % tf == 0
    return pl.pallas_call(
        _ffn_kernel,
        grid=(m // tm, f // tf),
        in_specs=[pl.BlockSpec((tm, d), lambda i, j: (i, 0)),
                  pl.BlockSpec((d, tf), lambda i, j: (0, j)),
                  pl.BlockSpec((d, tf), lambda i, j: (0, j)),
                  pl.BlockSpec((tf, d), lambda i, j: (j, 0)),
                  pl.BlockSpec((1, d), lambda i, j: (0, 0)),
                  pl.BlockSpec((1, d), lambda i, j: (0, 0))],
        out_specs=pl.BlockSpec((tm, d), lambda i, j: (i, 0)),
        out_shape=jax.ShapeDtypeStruct((m, d), F32),
        scratch_shapes=[pltpu.VMEM((tm, d), BF16), pltpu.VMEM((tm, d), F32)],
        compiler_params=_cparams("parallel", "arbitrary"),
    )(x, wg, wu, wd, g.reshape(1, d), b.reshape(1, d))


def _even_kernel(pa_ref, pb_ref, shift0_ref, sret0_ref, swkv0_ref,
                 mu_ref, vec_ref, w2p_ref, a2p_ref, g2_ref, hsum_ref, lincl_ref, usuf_ref,
                 dmask_ref, qdec_ref, kdec_ref, cdec_ref,
                 oret_ref, ob_ref, sret_ref, swkv_ref, shift_ref,
                 sret_s, swkv_s, prev_s,
                 at_s, bt_s, kt_s, rt_s, bg_s, kg_s, v_s, gc_s, q_s, k_s, vr_s,
                 yret_s, ywkv_s, bonus_s, gate_s,
                 t_s, arb_s, wl_s, yl_s, sl_s, ol_s, rl_s):
    c = pl.program_id(1)
    tb = pb_ref.shape[1]
    n_chunks = tb // CHUNK

    @pl.when(c == 0)
    def _():
        sret_s[...] = sret0_ref[0]
        swkv_s[...] = swkv0_ref[0]
        prev_s[...] = shift0_ref[0]

    hsum = hsum_ref[...]

    def vec(i):
        return vec_ref[i:i + 1, :]

    w0, a0, k_k, k_a, r_k, g_ret, g_wkv = (vec(i) for i in range(7))

    pb = pb_ref[0]
    row = lax.broadcasted_iota(I32, (tb, 1), 0)
    prev = jnp.where(row == 0, prev_s[...], pltpu.roll(pb, 1, axis=0))
    prev_s[...] = pb[tb - 1:tb, :]
    mixed = pb + (prev - pb) * mu_ref[...]
    rb = mixed[:, 0:W_MIX]
    kb = mixed[:, W_MIX:2 * W_MIX]
    vb = mixed[:, 2 * W_MIX:3 * W_MIX]
    wa = mixed[:, 3 * W_MIX:3 * W_MIX + LANES]
    gl = mixed[:, 3 * W_MIX + LANES:3 * W_MIX + 2 * LANES]
    lane = lax.broadcasted_iota(I32, (1, LANES), 1)
    wa_t = jnp.where(lane < DECAY_LORA, _tanh(wa), wa).astype(BF16)
    w_log = -_softplus(-(w0 + jnp.dot(wa_t, w2p_ref[...], preferred_element_type=F32))) - 0.5
    logw = -jnp.exp(w_log)
    lr = _sigmoid(a0 + jnp.dot(wa_t, a2p_ref[...], preferred_element_type=F32))
    gate_s[...] = jnp.dot(_sigmoid(gl).astype(BF16), g2_ref[...], preferred_element_type=F32)
    kkv = kb * k_k
    kkn = kkv * lax.rsqrt(jnp.maximum(_split_dot(kkv * kkv, hsum), 1e-12))
    k4 = kb * (1.0 + (lr - 1.0) * k_a)
    bonus_s[...] = _split_dot(rb * k4 * r_k, hsum) * vb
    cum = _split_dot_left(lincl_ref[...], logw)
    suf = _split_dot_left(usuf_ref[...], logw)
    ginv = jnp.exp(-cum)
    gsuf = jnp.exp(suf)
    at_s[...] = (-kkn * jnp.exp(cum - logw)).astype(BF16)
    bt_s[...] = (kkn * lr * ginv).astype(BF16)
    kt_s[...] = (k4 * ginv).astype(BF16)
    rt_s[...] = (rb * jnp.exp(cum)).astype(BF16)
    bg_s[...] = (kkn * lr * gsuf).astype(BF16)
    kg_s[...] = (k4 * gsuf).astype(BF16)
    v_s[...] = vb.astype(BF16)
    gc_s[...] = jnp.exp(cum + suf)

    q_s[...] = pa_ref[0, :, 0:W_MIX].astype(BF16)
    k_s[...] = (pa_ref[0, :, W_MIX:2 * W_MIX] * HEAD_DIM ** -0.5).astype(BF16)
    vr_s[...] = pa_ref[0, :, 2 * W_MIX:3 * W_MIX].astype(BF16)

    head0 = lane < HEAD_DIM
    r2 = lax.broadcasted_iota(I32, (2 * CHUNK, 2 * CHUNK), 0)
    c2 = lax.broadcasted_iota(I32, (2 * CHUNK, 2 * CHUNK), 1)
    same = (r2 // CHUNK) == (c2 // CHUNK)
    strict = same & (r2 > c2)
    incl = same & (r2 >= c2)

    eye = (r2 == c2).astype(F32)
    pairs = range(N_PAIRS)
    lanes_of = [slice(LANES * p, LANES * (p + 1)) for p in pairs]

    def stacked(ref, rows):
        out = []
        for p in pairs:
            x = ref[rows, lanes_of[p]]
            z = jnp.zeros_like(x)
            out.append(jnp.concatenate([jnp.where(head0, x, z), jnp.where(head0, z, x)], axis=0))
        return out

    def local_terms(ci, carry):
        rows = pl.ds(pl.multiple_of(ci * CHUNK, CHUNK), CHUNK)
        a2, b2, k2, rr2 = stacked(at_s, rows), stacked(bt_s, rows), stacked(kt_s, rows), stacked(rt_s, rows)
        v2, kg2 = stacked(v_s, rows), stacked(kg_s, rows)
        g = [_dot_nt(jnp.concatenate([a2[p], rr2[p]], axis=0), jnp.concatenate([b2[p], k2[p]], axis=0))
             for p in pairs]
        npow = [jnp.where(strict, g[p][0:128, 0:128], 0.0).astype(BF16) for p in pairs]
        a_ak = [jnp.where(strict, g[p][0:128, 128:256], 0.0) for p in pairs]
        a_rk = [jnp.where(incl, g[p][128:256, 128:256], 0.0) for p in pairs]
        for p in pairs:
            arb_s[ci * N_PAIRS + p] = jnp.where(incl, g[p][128:256, 0:128], 0.0).astype(BF16)
            wl_s[ci * N_PAIRS + p] = _dot(a_ak[p], v2[p])
            yl_s[ci * N_PAIRS + p] = _dot(a_rk[p], v2[p])
            sl_s[ci * N_PAIRS + p] = _dot(v2[p].astype(F32).T, kg2[p])
        tinv = [eye + npow[p].astype(F32) for p in pairs]
        for lvl in range(1, 6):
            npow = [_dot(npow[p], npow[p]).astype(BF16) for p in pairs]
            tinv = [tinv[p] + _dot(tinv[p], npow[p]) for p in pairs]
        for p in pairs:
            t_s[ci * N_PAIRS + p] = tinv[p].astype(BF16)
        q2, kr2, vr2 = stacked(q_s, rows), stacked(k_s, rows), stacked(vr_s, rows)
        att = [_dot_nt(q2[p], kr2[p]) * dmask_ref[p] for p in pairs]
        for p in pairs:
            ol_s[ci * N_PAIRS + p] = _dot(att[p], vr2[p])
            rl_s[ci * N_PAIRS + p] = _dot((kr2[p].astype(F32) * kdec_ref[p]).T, vr2[p])
        return carry

    lax.fori_loop(0, n_chunks, local_terms, 0)

    def recur(ci, carry):
        r0 = pl.multiple_of(ci * CHUNK, CHUNK)
        rows = pl.ds(r0, CHUNK)
        a2, rr2, bg2, q2 = stacked(at_s, rows), stacked(rt_s, rows), stacked(bg_s, rows), stacked(q_s, rows)
        s = [swkv_s[p] for p in pairs]
        sr = [sret_s[p] for p in pairs]
        ps = [_dot_nt(jnp.concatenate([a2[p], rr2[p]], axis=0), s[p]) for p in pairs]
        u = [_dot(t_s[ci * N_PAIRS + p], ps[p][0:128] + wl_s[ci * N_PAIRS + p]).astype(BF16) for p in pairs]
        for p in pairs:
            swkv_s[p] = (s[p] * gc_s[pl.ds(r0, 1), lanes_of[p]] + _dot(u[p].astype(F32).T, bg2[p])
                         + sl_s[ci * N_PAIRS + p])
        for p in pairs:
            y2 = ps[p][128:256] + _dot(arb_s[ci * N_PAIRS + p], u[p]) + yl_s[ci * N_PAIRS + p]
            ywkv_s[rows, lanes_of[p]] = y2[0:CHUNK] + y2[CHUNK:2 * CHUNK]
        for p in pairs:
            o2 = ol_s[ci * N_PAIRS + p] + _dot(q2[p], sr[p]) * qdec_ref[p]
            yret_s[rows, lanes_of[p]] = o2[0:CHUNK] + o2[CHUNK:2 * CHUNK]
            sret_s[p] = sr[p] * cdec_ref[p] + rl_s[ci * N_PAIRS + p]
        return carry

    lax.fori_loop(0, n_chunks, recur, 0)

    inv = 1.0 / HEAD_DIM
    yr = yret_s[...]
    d = yr - _split_dot(yr, hsum) * inv
    var = _split_dot(d * d, hsum) * inv
    ga = pa_ref[0, :, 3 * W_MIX:4 * W_MIX]
    oret_ref[0] = (d * lax.rsqrt(var + GN_EPS) * g_ret * (ga * _sigmoid(ga))).astype(oret_ref.dtype)
    yw = ywkv_s[...]
    d = yw - _split_dot(yw, hsum) * inv
    var = _split_dot(d * d, hsum) * inv
    ob_ref[0] = ((d * lax.rsqrt(var + WKV_GN_EPS) * g_wkv + bonus_s[...]) * gate_s[...]).astype(ob_ref.dtype)

    @pl.when(c == pl.num_programs(1) - 1)
    def _():
        sret_ref[0] = sret_s[...]
        swkv_ref[0] = swkv_s[...]
        shift_ref[0] = prev_s[...]


def _pair_blockdiag(s):
    b = s.shape[0]
    s = s.reshape(b, N_PAIRS, 2, HEAD_DIM, HEAD_DIM)
    z = jnp.zeros_like(s[:, :, 0])
    top = jnp.concatenate([s[:, :, 0], z], axis=-1)
    bot = jnp.concatenate([z, s[:, :, 1]], axis=-1)
    return jnp.concatenate([top, bot], axis=-2)


def _pair_unblock(s):
    b = s.shape[0]
    h0 = s[:, :, 0:HEAD_DIM, 0:HEAD_DIM]
    h1 = s[:, :, HEAD_DIM:, HEAD_DIM:]
    return jnp.stack([h0, h1], axis=2).reshape(b, N_HEADS, HEAD_DIM, HEAD_DIM)


def _retention_constants():
    log_g = jnp.log1p(-jnp.exp2(-5.0 - jnp.arange(N_HEADS, dtype=F32)))
    lg = jnp.repeat(log_g.reshape(N_PAIRS, 2), CHUNK, axis=1)
    t = jnp.tile(jnp.arange(CHUNK, dtype=F32), 2)
    hd = jnp.repeat(jnp.arange(2), CHUNK)
    diff = t[:, None] - t[None, :]
    same = hd[:, None] == hd[None, :]
    dmask = jnp.where(same[None] & (diff[None] >= 0),
                      jnp.exp(lg[:, :, None] * jnp.maximum(diff, 0.0)[None]), 0.0)
    ones = jnp.ones((1, 1, LANES), F32)
    qdec = jnp.exp(lg * (t[None] + 1.0))[:, :, None] * ones
    kdec = jnp.exp(lg * (CHUNK - 1.0 - t[None]))[:, :, None] * ones
    cdec = jnp.exp(lg * CHUNK)[:, :, None] * ones
    return dmask, qdec, kdec, cdec


def _chunk_sum_matrices(tb):
    r = jnp.arange(tb)
    same = (r[:, None] // CHUNK) == (r[None, :] // CHUNK)
    lincl = (same & (r[None, :] <= r[:, None])).astype(BF16)
    usuf = (same & (r[None, :] > r[:, None])).astype(BF16)
    return lincl, usuf


def _even_mixer(pa, pb, shift0, sret0, swkv0, prm, tb):
    b, t, _ = pa.shape
    assert t % tb == 0 and tb % CHUNK == 0
    lincl, usuf = _chunk_sum_matrices(tb)
    dmask, qdec, kdec, cdec = _retention_constants()
    hd = jnp.arange(W_MIX) // HEAD_DIM
    hsum = (hd[:, None] == hd[None, :]).astype(BF16)

    def fixed(shape):
        zero = (0,) * len(shape)
        return pl.BlockSpec(shape, lambda i, c: zero)

    def per_batch(shape):
        zero = (0,) * len(shape)
        return pl.BlockSpec((1,) + shape, lambda i, c: (i,) + zero)

    def per_block(w):
        return pl.BlockSpec((1, tb, w), lambda i, c: (i, c, 0))

    pair_state = (N_PAIRS, LANES, LANES)
    wide = lambda dt: pltpu.VMEM((tb, W_MIX), dt)
    per_chunk = lambda dt: pltpu.VMEM((tb // CHUNK * N_PAIRS, LANES, LANES), dt)
    outs = pl.pallas_call(
        _even_kernel,
        grid=(b, t // tb),
        in_specs=[per_block(4 * W_MIX), per_block(B_SHIFT_WIDTH),
                  per_batch((1, B_SHIFT_WIDTH)), per_batch(pair_state), per_batch(pair_state),
                  fixed((1, B_SHIFT_WIDTH)), fixed((8, W_MIX)),
                  fixed((LANES, W_MIX)), fixed((LANES, W_MIX)), fixed((GATE_LORA, W_MIX)),
                  fixed((W_MIX, W_MIX)), fixed((tb, tb)), fixed((tb, tb)),
                  fixed(pair_state), fixed(pair_state), fixed(pair_state), fixed(pair_state)],
        out_specs=[per_block(W_MIX), per_block(W_MIX),
                   per_batch(pair_state), per_batch(pair_state), per_batch((1, B_SHIFT_WIDTH))],
        out_shape=[jax.ShapeDtypeStruct((b, t, W_MIX), BF16), jax.ShapeDtypeStruct((b, t, W_MIX), BF16),
                   jax.ShapeDtypeStruct((b,) + pair_state, F32), jax.ShapeDtypeStruct((b,) + pair_state, F32),
                   jax.ShapeDtypeStruct((b, 1, B_SHIFT_WIDTH), F32)],
        scratch_shapes=[pltpu.VMEM(pair_state, F32), pltpu.VMEM(pair_state, F32),
                        pltpu.VMEM((1, B_SHIFT_WIDTH), F32)]
                       + [wide(BF16)] * 7 + [wide(F32)] + [wide(BF16)] * 3 + [wide(F32)] * 4
                       + [per_chunk(BF16)] * 2 + [per_chunk(F32)] * 5,
        compiler_params=_cparams("parallel", "arbitrary"),
    )(pa, pb, shift0, _pair_blockdiag(sret0), _pair_blockdiag(swkv0),
      prm["mu"], prm["vec"], prm["w2p"], prm["a2p"], prm["g2"], hsum, lincl, usuf,
      dmask, qdec, kdec, cdec)
    o_ret, o_b, sret, swkv, shift = outs
    return o_ret, o_b, shift, _pair_unblock(sret), _pair_unblock(swkv)


SUBLANES = 8


def _dsa_kernel(qc_ref, qi_ref, wit_ref, kc_ref, vat_ref, ki_ref, slope_ref, o_ref,
                key_s, qt_s, q3_s, acc_s, *, tq, tk, tka, past, topk):
    i = pl.program_id(1)
    q0 = past + i * tq
    n_kb = (q0 + tq + tk - 1) // tk

    qpos = q0 + lax.broadcasted_iota(I32, (1, tq), 1)
    qchunk = qpos >> LOG2_CHUNK
    sub_k = lax.broadcasted_iota(I32, (tk, 1), 0)
    sub_k_chunk = sub_k >> LOG2_CHUNK

    qt_s[...] = qc_ref[0].astype(F32).T.astype(BF16)
    qit_hi, qit_lo = _split_bf16(qi_ref[0].T)
    zero_rows = jnp.zeros((IDX_DIM, tq), BF16)
    for h in range(IDX_HEADS):
        rows = slice(IDX_DIM * h, IDX_DIM * (h + 1))
        q3_s[4 * IDX_DIM * h:4 * IDX_DIM * (h + 1), :] = jnp.concatenate(
            [qit_hi[rows], qit_lo[rows], qit_hi[rows], zero_rows], axis=0)

    wit = wit_ref[0] * (IDX_DIM ** -0.5 * IDX_HEADS ** -0.5)

    def score_body(kb, carry):
        k0 = pl.multiple_of(kb * tk, tk)
        ki = ki_ref[0, pl.ds(k0, tk), :]
        s = jnp.zeros((tk, tq), F32)
        for h in range(IDX_HEADS):
            qit = q3_s[4 * IDX_DIM * h:4 * IDX_DIM * (h + 1), :]
            s = s + jnp.maximum(jnp.dot(ki, qit, preferred_element_type=F32), 0.0) * wit[h:h + 1, :]
        s = s + 0.0
        adm = sub_k_chunk <= (qchunk - k0 // CHUNK)
        s = jnp.where(adm, s, -jnp.inf)
        bits = pltpu.bitcast(s, I32)
        key_s[pl.ds(k0, tk), :] = bits ^ ((bits >> 31) & 0x7FFFFFFF)
        return carry

    lax.fori_loop(0, n_kb, score_body, 0)

    def count(pred):
        def body(kb, acc):
            k0 = pl.multiple_of(kb * tk, tk)
            hit = jnp.where(pred(key_s[pl.ds(k0, tk), :], k0 + sub_k), 1.0, 0.0)
            parts = [hit[SUBLANES * g:SUBLANES * (g + 1), :] for g in range(tk // SUBLANES)]
            while len(parts) > 1:
                parts = [parts[n] + parts[n + 1] for n in range(0, len(parts), 2)]
            return acc + parts[0]
        acc = lax.fori_loop(0, n_kb, body, jnp.zeros((SUBLANES, tq), F32))
        return jnp.sum(acc, axis=0, keepdims=True)

    base = jnp.where(count(lambda key, kp: key >= 0) >= topk, 0, INT_MIN).astype(I32)

    def bit_body(b, base):
        cand = base | (jnp.int32(1) << (30 - b))
        return jnp.where(count(lambda key, kp: key >= cand) >= topk, cand, base)

    thr = lax.fori_loop(0, 31, bit_body, base)
    need = topk - count(lambda key, kp: key > thr)
    n_eq = count(lambda key, kp: key == thr)

    def tie_cut(_):
        def tbit(b, m):
            cand = m | (jnp.int32(1) << (14 - b))
            return jnp.where(count(lambda key, kp: (key == thr) & (kp < cand)) < need, cand, m)
        return lax.fori_loop(0, 15, tbit, jnp.zeros((1, tq), I32))

    last_eq = lax.cond(jnp.max(n_eq - need) > 0.0, tie_cut,
                       lambda _: jnp.full((1, tq), 2 ** 30, I32), 0)

    acc_s[...] = jnp.zeros(acc_s.shape, F32)
    sub_a = lax.broadcasted_iota(I32, (tka, 1), 0)
    sub_a_chunk = sub_a >> LOG2_CHUNK
    sub_af = sub_a.astype(F32)
    qposf = qpos.astype(F32)
    n_kba = (q0 + tq + tka - 1) // tka
    heads = range(N_HEADS)

    def att_body(kb, m):
        k0 = pl.multiple_of(kb * tka, tka)
        kpos = k0 + sub_a
        key = key_s[pl.ds(k0, tka), :]
        sel = ((sub_a_chunk <= (qchunk - k0 // CHUNK))
               & ((key > thr) | ((key == thr) & (kpos <= last_eq))))
        ndist = -jnp.abs(qposf - (sub_af + k0.astype(F32)))
        kc = kc_ref[0, pl.ds(k0, tka), :]
        vat = vat_ref[0, :, pl.ds(k0, tka)]
        dots = [jnp.dot(kc, qt_s[HEAD_DIM * h:HEAD_DIM * (h + 1), :], preferred_element_type=F32)
                for h in heads]
        logit = [jnp.where(sel, dots[h] + slope_ref[h] * ndist, NEG_BIG) for h in heads]
        m_old = [m[h:h + 1, :] for h in heads]
        m_new = [jnp.maximum(m_old[h], jnp.max(logit[h], axis=0, keepdims=True)) for h in heads]
        p = [jnp.exp2(logit[h] - m_new[h]).astype(BF16) for h in heads]
        pv = [jnp.dot(vat, p[h], preferred_element_type=F32) for h in heads]
        for h in heads:
            acc_s[h] = jnp.exp2(m_old[h] - m_new[h]) * acc_s[h] + pv[h]
        return jnp.concatenate(m_new, axis=0)

    lax.fori_loop(0, n_kba, att_body, jnp.full((N_HEADS, tq), NEG_BIG, F32))
    outs = []
    for h in heads:
        acc = acc_s[h]
        outs.append(acc[0:HEAD_DIM, :] / acc[HEAD_DIM:HEAD_DIM + 1, :])
    o_ref[0] = jnp.concatenate(outs, axis=0).T.astype(o_ref.dtype)


def _dsa(pq, qi, wi, kc, vc, ki, past, tq, tk, tka):
    b, t, _ = pq.shape
    lp = kc.shape[1]
    ki_hi, ki_lo = _split_bf16(ki)
    ki3 = jnp.concatenate([ki_hi, ki_hi, ki_lo, jnp.zeros_like(ki_hi)], axis=-1)
    n_keys = past + t
    topk = min(TOPK_MAX, n_keys // 4)
    assert t % tq == 0 and tq % CHUNK == 0 and lp % tk == 0 and tk % tka == 0 and lp >= n_keys
    slopes = jnp.exp2(-8.0 * (jnp.arange(N_HEADS, dtype=F32) + 1.0) / N_HEADS) * LOG2E
    vat = jnp.concatenate([jnp.swapaxes(vc, 1, 2), jnp.ones((b, 1, lp), BF16),
                           jnp.zeros((b, LANES - HEAD_DIM - 1, lp), BF16)], axis=1)
    wit = jnp.swapaxes(wi, 1, 2)
    qblk = lambda w, col: pl.BlockSpec((1, tq, w), lambda bi, i: (bi, i, col))
    kblk = lambda w: pl.BlockSpec((1, lp, w), lambda bi, i: (bi, 0, 0))
    return pl.pallas_call(
        functools.partial(_dsa_kernel, tq=tq, tk=tk, tka=tka, past=past, topk=topk),
        grid=(b, t // tq),
        in_specs=[qblk(W_MIX, 0), qblk(W_MIX, 0),
                  pl.BlockSpec((1, IDX_HEADS, tq), lambda bi, i: (bi, 0, i)),
                  kblk(HEAD_DIM), pl.BlockSpec((1, LANES, lp), lambda bi, i: (bi, 0, 0)),
                  kblk(4 * IDX_DIM), pl.BlockSpec(memory_space=pltpu.SMEM)],
        out_specs=qblk(W_MIX, 0),
        out_shape=jax.ShapeDtypeStruct((b, t, W_MIX), BF16),
        scratch_shapes=[pltpu.VMEM((lp, tq), I32), pltpu.VMEM((W_MIX, tq), BF16),
                        pltpu.VMEM((4 * IDX_DIM * IDX_HEADS, tq), BF16),
                        pltpu.VMEM((N_HEADS, LANES, tq), F32)],
        compiler_params=_cparams("parallel", "arbitrary"),
    )(pq, qi, wit, kc, vat, ki3, slopes)


def _sb_kernel(q_ref, k_ref, v_ref, o_ref, *, tq, tk, past):
    i = pl.program_id(2)
    q0 = past + i * tq
    qpos = q0 + lax.broadcasted_iota(I32, (tq, 1), 0)
    lane_k = lax.broadcasted_iota(I32, (1, tk), 1)
    lane = lax.broadcasted_iota(I32, (1, LANES), 1)
    head0 = lane < HEAD_DIM
    rk = lax.broadcasted_iota(I32, (tk, tk), 0)
    ck = lax.broadcasted_iota(I32, (tk, tk), 1)
    later = (rk > ck).astype(BF16)

    q = q_ref[0]
    zq = jnp.zeros_like(q)
    q_heads = (jnp.where(head0, q, zq), jnp.where(head0, zq, q))

    j_hi = (q0 + tq - 2) // tk
    n_full = q0 // tk

    def blocks(js, carry):
        heads = range(2)
        k0 = [pl.multiple_of(j * tk, tk) for j, _ in js]
        kblk = [k_ref[0, pl.ds(k, tk), :] for k in k0]
        vblk = [v_ref[0, pl.ds(k, tk), :] for k in k0]
        z = [[_dot_nt(q_heads[h], kb) for h in heads] for kb in kblk]
        lrem, lsig, causal = [], [], []
        for n, (_, masked) in enumerate(js):
            causal.append((k0[n] + lane_k) < qpos if masked else None)
            row_r, row_s = [], []
            for h in heads:
                r = -(jnp.maximum(z[n][h], 0.0) + jnp.log(1.0 + jnp.exp(-jnp.abs(z[n][h]))))
                row_s.append(z[n][h] + r)
                row_r.append(jnp.where(causal[n], r, 0.0) if masked else r)
            lrem.append(row_r)
            lsig.append(row_s)
        run = [carry[h][0] for h in heads]
        acc = [carry[h][1] for h in heads]
        run_in = []
        for n in range(len(js)):
            run_in.append(list(run))
            run = [run[h] + jnp.sum(lrem[n][h], axis=1, keepdims=True) for h in heads]
        after = [[jnp.dot(lrem[n][h].astype(BF16), later, preferred_element_type=F32) for h in heads]
                 for n in range(len(js))]
        for n, (_, masked) in enumerate(js):
            for h in heads:
                att = jnp.exp(lsig[n][h] + after[n][h] + run_in[n][h])
                if masked:
                    att = jnp.where(causal[n], att, 0.0)
                acc[h] = acc[h] + _dot(att, vblk[n])
        return tuple((run[h], acc[h]) for h in heads)

    init = tuple((jnp.zeros((tq, 1), F32), jnp.zeros((tq, LANES), F32)) for _ in range(2))
    carry = lax.cond(n_full > 0,
                     lambda: blocks([(j_hi, True), (j_hi - 1, False)], init),
                     lambda: blocks([(j_hi, True)], init))

    def more(state):
        s, c = state
        return (s < n_full) & (jnp.max(jnp.maximum(c[0][0], c[1][0])) > SB_EXIT_LOG)

    _, carry = lax.while_loop(more, lambda st: (st[0] + 1, blocks([(n_full - 1 - st[0], False)], st[1])),
                              (jnp.int32(1), carry))
    o_ref[0] = jnp.where(head0, carry[0][1], carry[1][1]).astype(o_ref.dtype)


def _stickbreak(q, q_col, k, v, kv_cols, past, tq, tk):
    b, t, _ = q.shape
    lp = k.shape[1]
    assert t % tq == 0 and lp % tk == 0 and lp >= past + t
    assert tk % tq == 0 and past % tq == 0
    k_col, v_col = kv_cols
    return pl.pallas_call(
        functools.partial(_sb_kernel, tq=tq, tk=tk, past=past),
        grid=(b, N_PAIRS, t // tq),
        in_specs=[pl.BlockSpec((1, tq, LANES), lambda bi, p, i: (bi, i, q_col + p)),
                  pl.BlockSpec((1, lp, LANES), lambda bi, p, i: (bi, 0, k_col + p)),
                  pl.BlockSpec((1, lp, LANES), lambda bi, p, i: (bi, 0, v_col + p))],
        out_specs=pl.BlockSpec((1, tq, LANES), lambda bi, p, i: (bi, i, p)),
        out_shape=jax.ShapeDtypeStruct((b, t, W_MIX), BF16),
        compiler_params=_cparams("parallel", "parallel", "arbitrary"),
    )(q, k, v)


def _row_tile(m, cap):
    t = cap
    while m % t:
        t //= 2
    return t


def _even_layer(x, shift0, sret0, swkv0, w_in, w_out, prm, tb):
    b, t, d = x.shape
    xf = x.reshape(b * t, d)
    tm = _row_tile(b * t, 1024)
    (pa,) = _proj(xf, w_in[:, :4 * W_MIX], (F32,), tm, 2 * W_MIX)
    (pb,) = _proj(xf, w_in[:, 4 * W_MIX:], (F32,), tm, B_SHIFT_WIDTH // 2)
    o_ret, o_b, shift, sret, swkv = _even_mixer(pa.reshape(b, t, -1), pb.reshape(b, t, -1),
                                                shift0, sret0, swkv0, prm, tb)
    return o_ret.reshape(b * t, -1), o_b.reshape(b * t, -1), w_out, (shift, sret, swkv)


def _pad_rows(a, lp):
    return jnp.pad(a, ((0, 0), (0, lp - a.shape[1]), (0, 0)))


def _odd_layer(x, caches, w_q, w_kv, w_small, w_sel, w_out, tq_dsa, tq_sb, tk_dsa, tka_dsa, tk_sb):
    b, t, d = x.shape
    xf = x.reshape(b * t, d)
    tm = _row_tile(b * t, 1024)
    (pq16,) = _proj(xf, w_q, (BF16,), tm, w_q.shape[1] // 2)
    pkv32, pkv16 = _proj(xf, w_kv, (F32, BF16), tm, w_kv.shape[1] // 2)
    (ps32,) = _proj(xf, w_small, (F32,), tm, w_small.shape[1])
    qi, sel_small = _proj3(xf, w_sel, IDX_HEADS * IDX_DIM, tm)
    pq16 = pq16.reshape(b, t, -1)
    pkv32 = pkv32.reshape(b, t, -1)
    pkv16 = pkv16.reshape(b, t, -1)
    ps32 = ps32.reshape(b, t, -1)
    sel_small = sel_small.reshape(b, t, -1)
    kc, vc = ps32[..., :HEAD_DIM], ps32[..., HEAD_DIM:]
    ki = sel_small[..., :IDX_DIM]
    wi = sel_small[..., IDX_DIM:IDX_DIM + IDX_HEADS]
    new_rows = (kc, vc, ki, pkv32[..., :W_MIX].reshape(b, t, N_HEADS, HEAD_DIM),
                pkv32[..., W_MIX:].reshape(b, t, N_HEADS, HEAD_DIM))
    kc_a, vc_a, ki_a = kc.astype(BF16), vc.astype(BF16), ki
    if caches is None:
        past = 0
        k_sb = v_sb = pkv16
        sb_cols = (0, N_PAIRS)
    else:
        past = caches[0].shape[1]
        kc_a = jnp.concatenate([caches[0].astype(BF16), kc_a], axis=1)
        vc_a = jnp.concatenate([caches[1].astype(BF16), vc_a], axis=1)
        ki_a = jnp.concatenate([caches[2], ki_a], axis=1)
        lp_sb = -(-(past + t) // tk_sb) * tk_sb
        tail = jnp.zeros((b, lp_sb - past - t, W_MIX), BF16)
        k_sb, v_sb = (jnp.concatenate([c.reshape(b, past, -1).astype(BF16), n, tail], axis=1)
                      for c, n in zip(caches[3:], (pkv16[..., :W_MIX], pkv16[..., W_MIX:])))
        sb_cols = (0, 0)
    n_keys = past + t
    lp_dsa = -(-n_keys // tk_dsa) * tk_dsa
    kc_a, vc_a, ki_a = (_pad_rows(a, lp_dsa) for a in (kc_a, vc_a, ki_a))
    oc = _dsa(pq16, qi.reshape(b, t, -1), wi, kc_a, vc_a, ki_a, past, tq_dsa, tk_dsa, tka_dsa)
    od = _stickbreak(pq16, N_PAIRS, k_sb, v_sb, sb_cols, past, tq_sb, tk_sb)
    return oc.reshape(b * t, -1), od.reshape(b * t, -1), w_out, new_rows


def kernel(x_prompt, x_sample, state_shift, state_ret, state_wkv, cache_dsa_k, cache_dsa_v, cache_idx_k,
           cache_sb_k, cache_sb_v, w_in_even, w_out_even, gn_ret_g, rwkv_mu, rwkv_w0, rwkv_w2, rwkv_a0,
           rwkv_a2, rwkv_g2, rwkv_kk, rwkv_ka, rwkv_rk, gn_wkv_g, w_in_odd, w_out_odd, ln1_g, ln1_b,
           ln2_g, ln2_b, ffn_wg, ffn_wu, ffn_wd):
    xp, xs = x_prompt, x_sample
    bp, tp, d = xp.shape
    bs, ts, _ = xs.shape
    p_even, s_even = [[], [], []], [[], [], []]
    p_odd, s_odd = [[] for _ in range(5)], [[] for _ in range(5)]

    for l in range(DEPTH):
        if l % 2 == 0:
            e = l // 2
            zpad = jnp.zeros((DECAY_LORA, W_MIX), F32)
            prm = dict(
                mu=rwkv_mu[e].reshape(1, -1),
                vec=jnp.stack([rwkv_w0[e], rwkv_a0[e], rwkv_kk[e], rwkv_ka[e], rwkv_rk[e], gn_ret_g[e],
                               gn_wkv_g[e], jnp.zeros((W_MIX,), F32)]),
                w2p=jnp.concatenate([rwkv_w2[e], zpad], axis=0).astype(BF16),
                a2p=jnp.concatenate([zpad, rwkv_a2[e]], axis=0).astype(BF16),
                g2=rwkv_g2[e].astype(BF16))
            w_in = w_in_even[e].astype(BF16)
            w_out = w_out_even[e].astype(BF16)
            zeros = lambda *s: jnp.zeros(s, F32)
            o1p, o2p, wo, st_p = _even_layer(xp, zeros(bp, 1, B_SHIFT_WIDTH),
                                             zeros(bp, N_HEADS, HEAD_DIM, HEAD_DIM),
                                             zeros(bp, N_HEADS, HEAD_DIM, HEAD_DIM), w_in, w_out, prm, 256)
            o1s, o2s, _, st_s = _even_layer(xs, state_shift[e], state_ret[e], state_wkv[e], w_in, w_out,
                                            prm, ts)
            for n in range(3):
                p_even[n].append(st_p[n])
                s_even[n].append(st_s[n])
        else:
            o = l // 2
            w = w_in_odd[o]
            seg = {}
            off = 0
            for name, sz in (("qc", W_MIX), ("kc", HEAD_DIM), ("vc", HEAD_DIM), ("qi", IDX_HEADS * IDX_DIM),
                             ("ki", IDX_DIM), ("wi", IDX_HEADS), ("qd", W_MIX), ("kd", W_MIX), ("vd", W_MIX)):
                seg[name] = w[:, off:off + sz]
                off += sz
            q_scale = HEAD_DIM ** -0.5
            w_q = jnp.concatenate([seg["qc"] * (q_scale * LOG2E), seg["qd"] * q_scale], axis=1).astype(BF16)
            w_kv = jnp.concatenate([seg["kd"], seg["vd"]], axis=1).astype(BF16)
            w_small = jnp.concatenate([seg["kc"], seg["vc"]], axis=1).astype(BF16)
            w_sel = jnp.concatenate([seg["qi"], seg["ki"], seg["wi"],
                                     jnp.zeros((d, LANES - IDX_DIM - IDX_HEADS), F32)], axis=1)
            w_out = w_out_odd[o].astype(BF16)
            o1p, o2p, wo, rows_p = _odd_layer(xp, None, w_q, w_kv, w_small, w_sel, w_out,
                                              256, 256, 512, 256, 256)
            o1s, o2s, _, rows_s = _odd_layer(xs, (cache_dsa_k[o], cache_dsa_v[o], cache_idx_k[o],
                                                  cache_sb_k[o], cache_sb_v[o]),
                                             w_q, w_kv, w_small, w_sel, w_out, ts, ts, 512, 256, 256)
            for n in range(5):
                p_odd[n].append(rows_p[n])
                s_odd[n].append(rows_s[n])

        xpf = _outproj_ln(o1p, o2p, wo, xp.reshape(bp * tp, d), ln1_g[l], ln1_b[l], 512)
        xsf = _outproj_ln(o1s, o2s, wo, xs.reshape(bs * ts, d), ln1_g[l], ln1_b[l], 512)
        wg, wu, wd = ffn_wg[l].astype(BF16), ffn_wu[l].astype(BF16), ffn_wd[l].astype(BF16)
        xp = _ffn_ln(xpf, wg, wu, wd, ln2_g[l], ln2_b[l], 1024, 256).reshape(bp, tp, d)
        xs = _ffn_ln(xsf, wg, wu, wd, ln2_g[l], ln2_b[l], 1024, 256).reshape(bs, ts, d)

    stack = lambda lists: [jnp.stack(a) for a in lists]
    return (xp, xs, *stack(p_even), *stack(p_odd), *stack(s_even), *stack(s_odd))
```
